```python
import math
import jax, jax.numpy as jnp
from jax import lax
import numpy as np

D_MODEL = 1024
BATCH = 32
SEQ = 2048
DEPTH = 2

HEAD_DIM = 64
SB_HEADS = 8
MB_HEADS = 8
SB_W = SB_HEADS * HEAD_DIM
MB_W = MB_HEADS * HEAD_DIM
CONV_CH = D_MODEL // 2
CONV_WIDTH = 31
MOBA_BLOCK = 256
MOBA_TOPK = 3
Q_BLOCK = 128
MOBA_Q_CHUNK = 32
D_FF = 2816
N_BRANCH = 3
EPS = 1e-6

OFF_SBQ = 0
OFF_SBK = OFF_SBQ + SB_W
OFF_SBV = OFF_SBK + SB_W
OFF_MBQ = OFF_SBV + SB_W
OFF_MBK = OFF_MBQ + MB_W
OFF_MBV = OFF_MBK + MB_W
OFF_CONV = OFF_MBV + MB_W
OFF_GATE = OFF_CONV + 2 * CONV_CH
IN_COLS = OFF_GATE + N_BRANCH * D_MODEL

kernel_name = "hybrid_stickbreak_moba_conformer_macaron"


def rmsnorm(x, g):
    xf = x.astype(jnp.float32)
    y = xf * lax.rsqrt(jnp.mean(xf * xf, axis=-1, keepdims=True) + EPS)
    return (y * g.astype(jnp.float32)).astype(x.dtype)


def layernorm(x, g, b):
    xf = x.astype(jnp.float32)
    mu = jnp.mean(xf, axis=-1, keepdims=True)
    var = jnp.mean(jnp.square(xf - mu), axis=-1, keepdims=True)
    y = (xf - mu) * lax.rsqrt(var + EPS)
    return (y * g.astype(jnp.float32) + b.astype(jnp.float32)).astype(x.dtype)


def swiglu(x, w_in, w_out):
    a = x @ w_in
    return (jax.nn.silu(a[..., :D_FF]) * a[..., D_FF:]) @ w_out


def to_heads(a, n_heads):
    b, s, _ = a.shape
    return a.reshape(b, s, n_heads, HEAD_DIM).transpose(0, 2, 1, 3)


def from_heads(a):
    b, h, s, d = a.shape
    return a.transpose(0, 2, 1, 3).reshape(b, s, h * d)


def stick_breaking_attention(q, k, v):
    B, H, S, Dh = q.shape
    n_blk = -(-S // Q_BLOCK)
    s_pad = n_blk * Q_BLOCK
    qp = jnp.pad(q, ((0, 0), (0, 0), (0, s_pad - S), (0, 0)))
    scale = HEAD_DIM ** -0.5
    key_pos = jnp.arange(S)

    def block(start):
        qb = lax.dynamic_slice_in_dim(qp, start, Q_BLOCK, axis=2)
        t = start + jnp.arange(Q_BLOCK)
        z = jnp.einsum('bhqd,bhkd->bhqk', qb, k).astype(jnp.float32) * scale
        past = key_pos[None, :] < t[:, None]
        log_beta = jax.nn.log_sigmoid(z)
        log_1m = jnp.where(past, jax.nn.log_sigmoid(-z), 0.0)
        between = lax.cumsum(log_1m, axis=3, reverse=True) - log_1m
        w = jnp.where(past, jnp.exp(log_beta + between), 0.0)
        return jnp.einsum('bhqk,bhkd->bhqd', w.astype(v.dtype), v)

    starts = jnp.arange(n_blk, dtype=jnp.int32) * Q_BLOCK
    out = lax.map(block, starts)
    out = out.transpose(1, 2, 0, 3, 4).reshape(B, H, s_pad, Dh)
    return out[:, :, :S]


def moba_attention(q, k, v, slopes):
    B, H, S, Dh = q.shape
    nb = -(-S // MOBA_BLOCK)
    s_pad = nb * MOBA_BLOCK
    pad = ((0, 0), (0, 0), (0, s_pad - S), (0, 0))
    qp, kp, vp = jnp.pad(q, pad), jnp.pad(k, pad), jnp.pad(v, pad)
    kblk = kp.reshape(B, H, nb, MOBA_BLOCK, Dh)
    vblk = vp.reshape(B, H, nb, MOBA_BLOCK, Dh)
    kmean = jnp.mean(kblk.astype(jnp.float32), axis=3)
    n_sel = min(MOBA_TOPK, nb - 1)
    scale = HEAD_DIM ** -0.5
    bi = jnp.arange(B)[:, None, None, None]
    hi = jnp.arange(H)[None, :, None, None]
    blk_pos = jnp.arange(MOBA_BLOCK)

    def chunk(start):
        own = start // MOBA_BLOCK
        qb = lax.dynamic_slice_in_dim(qp, start, MOBA_Q_CHUNK, axis=2)
        t = start + jnp.arange(MOBA_Q_CHUNK)
        k_own = lax.dynamic_index_in_dim(kblk, own, axis=2, keepdims=False)
        v_own = lax.dynamic_index_in_dim(vblk, own, axis=2, keepdims=False)
        s_own = own * MOBA_BLOCK + blk_pos
        dist_own = (t[:, None] - s_own[None, :]).astype(jnp.float32)
        sc_own = (jnp.einsum('bhqd,bhkd->bhqk', qb, k_own).astype(jnp.float32) * scale
                  - slopes[:, None, None] * dist_own)
        sc_own = jnp.where(s_own[None, :] <= t[:, None], sc_own, -jnp.inf)
        if n_sel > 0:
            gate = jnp.einsum('bhqd,bhnd->bhqn', qb.astype(jnp.float32), kmean)
            gate = jnp.where(jnp.arange(nb) < own, gate, -jnp.inf)
            _, idx = lax.top_k(gate, n_sel)
            k_sel = kblk[bi, hi, idx]
            v_sel = vblk[bi, hi, idx]
            s_sel = idx[..., None] * MOBA_BLOCK + blk_pos
            dist_sel = (t[:, None, None] - s_sel).astype(jnp.float32)
            sc_sel = (jnp.einsum('bhqd,bhqnkd->bhqnk', qb, k_sel).astype(jnp.float32) * scale
                      - slopes[:, None, None, None] * dist_sel)
            sc_sel = jnp.where((idx < own)[..., None], sc_sel, -jnp.inf)
            n_k = n_sel * MOBA_BLOCK
            sc = jnp.concatenate([sc_sel.reshape(B, H, MOBA_Q_CHUNK, n_k), sc_own], axis=-1)
            p = jax.nn.softmax(sc, axis=-1).astype(v.dtype)
            p_sel = p[..., :n_k].reshape(B, H, MOBA_Q_CHUNK, n_sel, MOBA_BLOCK)
            o = (jnp.einsum('bhqnk,bhqnkd->bhqd', p_sel, v_sel)
                 + jnp.einsum('bhqk,bhkd->bhqd', p[..., n_k:], v_own))
        else:
            p = jax.nn.softmax(sc_own, axis=-1).astype(v.dtype)
            o = jnp.einsum('bhqk,bhkd->bhqd', p, v_own)
        return o

    n_chunks = s_pad // MOBA_Q_CHUNK
    starts = jnp.arange(n_chunks, dtype=jnp.int32) * MOBA_Q_CHUNK
    out = lax.map(chunk, starts)
    out = out.transpose(1, 2, 0, 3, 4).reshape(B, H, s_pad, Dh)
    return out[:, :, :S]


def conformer_conv(a, w_dw, b_dw, ln_g, ln_b, w_pw):
    h = a[..., :CONV_CH] * jax.nn.sigmoid(a[..., CONV_CH:])
    h = lax.conv_general_dilated(h, w_dw, window_strides=(1,),
                                 padding=[(CONV_WIDTH - 1, 0)],
                                 dimension_numbers=('NWC', 'WIO', 'NWC'),
                                 feature_group_count=CONV_CH) + b_dw
    h = jax.nn.silu(layernorm(h, ln_g, ln_b))
    return h @ w_pw


def alibi_slopes(n_heads):
    return jnp.exp2(-8.0 * jnp.arange(1, n_heads + 1, dtype=jnp.float32) / n_heads)


def setup_inputs(seed: int = 0) -> dict:
    key = jax.random.key(seed)
    ks = jax.random.split(key, 24)
    L, D = DEPTH, D_MODEL

    def nrm(k, shape, fan_in):
        return jax.random.normal(k, shape, jnp.float32) * (fan_in ** -0.5)

    def gain(k, shape):
        return 1.0 + 0.02 * jax.random.normal(k, shape, jnp.float32)

    def small(k, shape):
        return 0.01 * jax.random.normal(k, shape, jnp.float32)

    return {
        "x": jax.random.normal(ks[0], (BATCH, SEQ, D), jnp.float32),
        "ffn1_norm": gain(ks[1], (L, D)),
        "ffn1_w_in": nrm(ks[2], (L, D, 2 * D_FF), D),
        "ffn1_w_out": nrm(ks[3], (L, D_FF, D), D_FF),
        "mix_norm": gain(ks[4], (L, D)),
        "w_in": nrm(ks[5], (L, D, IN_COLS), D),
        "gate_bias": small(ks[6], (L, N_BRANCH * D)),
        "sb_w_out": nrm(ks[7], (L, SB_W, D), SB_W),
        "mb_w_out": nrm(ks[8], (L, MB_W, D), MB_W),
        "conv_dw": nrm(ks[9], (L, CONV_WIDTH, 1, CONV_CH), CONV_WIDTH),
        "conv_dw_bias": small(ks[10], (L, CONV_CH)),
        "conv_ln_g": gain(ks[11], (L, CONV_CH)),
        "conv_ln_b": small(ks[12], (L, CONV_CH)),
        "conv_w_out": nrm(ks[13], (L, CONV_CH, D), CONV_CH),
        "w_o": nrm(ks[14], (L, D, D), D),
        "ffn2_norm": gain(ks[15], (L, D)),
        "ffn2_w_in": nrm(ks[16], (L, D, 2 * D_FF), D),
        "ffn2_w_out": nrm(ks[17], (L, D_FF, D), D_FF),
        "final_norm": gain(ks[18], (D,)),
    }


def reference(x, ffn1_norm, ffn1_w_in, ffn1_w_out, mix_norm, w_in, gate_bias,
              sb_w_out, mb_w_out, conv_dw, conv_dw_bias, conv_ln_g, conv_ln_b,
              conv_w_out, w_o, ffn2_norm, ffn2_w_in, ffn2_w_out, final_norm):
    slopes = alibi_slopes(MB_HEADS)
    h = x
    for l in range(DEPTH):
        h = h + 0.5 * swiglu(rmsnorm(h, ffn1_norm[l]), ffn1_w_in[l], ffn1_w_out[l])
        u = rmsnorm(h, mix_norm[l])
        p = u @ w_in[l]
        q_a = to_heads(p[..., OFF_SBQ:OFF_SBK], SB_HEADS)
        k_a = to_heads(p[..., OFF_SBK:OFF_SBV], SB_HEADS)
        v_a = to_heads(p[..., OFF_SBV:OFF_MBQ], SB_HEADS)
        q_b = to_heads(p[..., OFF_MBQ:OFF_MBK], MB_HEADS)
        k_b = to_heads(p[..., OFF_MBK:OFF_MBV], MB_HEADS)
        v_b = to_heads(p[..., OFF_MBV:OFF_CONV], MB_HEADS)
        y_a = from_heads(stick_breaking_attention(q_a, k_a, v_a)) @ sb_w_out[l]
        y_b = from_heads(moba_attention(q_b, k_b, v_b, slopes)) @ mb_w_out[l]
        y_c = conformer_conv(p[..., OFF_CONV:OFF_GATE], conv_dw[l], conv_dw_bias[l],
                             conv_ln_g[l], conv_ln_b[l], conv_w_out[l])
        g = jax.nn.sigmoid(p[..., OFF_GATE:] + gate_bias[l])
        mixed = (g[..., :D_MODEL] * y_a + g[..., D_MODEL:2 * D_MODEL] * y_b
                 + g[..., 2 * D_MODEL:] * y_c)
        h = h + mixed @ w_o[l]
        h = h + 0.5 * swiglu(rmsnorm(h, ffn2_norm[l]), ffn2_w_in[l], ffn2_w_out[l])
    return rmsnorm(h, final_norm)
```

```python
import functools

import jax
import jax.numpy as jnp
from jax import lax
from jax.experimental import pallas as pl
from jax.experimental.pallas import tpu as pltpu

D_MODEL = 1024
HEAD_DIM = 64
N_HEADS = 8
ATT_W = N_HEADS * HEAD_DIM
CONV_CH = D_MODEL // 2
CONV_WIDTH = 31
MOBA_BLOCK = 256
MOBA_TOPK = 3
D_FF = 2816
N_BRANCH = 3
EPS = 1e-6
QK_SCALE = HEAD_DIM ** -0.5

OFF_SBQ = 0
OFF_SBK = OFF_SBQ + ATT_W
OFF_SBV = OFF_SBK + ATT_W
OFF_MBQ = OFF_SBV + ATT_W
OFF_MBK = OFF_MBQ + ATT_W
OFF_MBV = OFF_MBK + ATT_W
OFF_CONV = OFF_MBV + ATT_W
OFF_GATE = OFF_CONV + 2 * CONV_CH
IN_COLS = OFF_GATE + N_BRANCH * D_MODEL

LANES = 128
PAIR = LANES // HEAD_DIM
FF_CHUNK = 256
ROW_TILE = 512
ATT_TILE = 256
CONV_TILE = 64
CONV_HALO = 32
VMEM_LIMIT = 56 * 1024 * 1024
MASK_BIG = 2.0 ** 100
NEG = -1e30

F32 = jnp.float32
BF16 = jnp.bfloat16


def _rms(x, g):
    ms = jnp.mean(x * x, axis=-1, keepdims=True)
    return x * lax.rsqrt(ms + EPS) * g


def _resident(shape):
    nd = len(shape)
    return pl.BlockSpec(shape, lambda *_: (0,) * nd, pipeline_mode=pl.Buffered(1))


def _params(sem):
    return pltpu.CompilerParams(dimension_semantics=sem, vmem_limit_bytes=VMEM_LIMIT)


def _ffn_kernel(x_ref, g_ref, w1_ref, w2_ref, *rest, final):
    if final:
        gf_ref, o_ref, mid_ref = rest
    else:
        o_ref, mid_ref = rest
    x = x_ref[...]
    xn = _rms(x, g_ref[...]).astype(BF16)
    for c in range(D_FF // FF_CHUNK):
        lo = c * FF_CHUNK
        a = jnp.dot(xn, w1_ref[:, lo:lo + FF_CHUNK], preferred_element_type=F32)
        b = jnp.dot(xn, w1_ref[:, D_FF + lo:D_FF + lo + FF_CHUNK], preferred_element_type=F32)
        mid_ref[:, lo:lo + FF_CHUNK] = (a * jax.nn.sigmoid(a) * b).astype(BF16)
    y = x + 0.5 * jnp.dot(mid_ref[...], w2_ref[...], preferred_element_type=F32)
    if final:
        y = _rms(y, gf_ref[...])
    o_ref[...] = y


def _ffn(h, g, w1, w2, gf=None):
    n = h.shape[0]
    final = gf is not None
    row = pl.BlockSpec((ROW_TILE, D_MODEL), lambda i: (i, 0))
    in_specs = [row, _resident((1, D_MODEL)), _resident((D_MODEL, 2 * D_FF)),
                _resident((D_FF, D_MODEL))]
    args = [h, g, w1, w2]
    if final:
        in_specs.append(_resident((1, D_MODEL)))
        args.append(gf)
    return pl.pallas_call(
        functools.partial(_ffn_kernel, final=final),
        grid=(n // ROW_TILE,),
        in_specs=in_specs,
        out_specs=row,
        out_shape=jax.ShapeDtypeStruct((n, D_MODEL), F32),
        scratch_shapes=[pltpu.VMEM((ROW_TILE, D_FF), BF16)],
        compiler_params=_params(("parallel",)),
        name="ffn_final" if final else "ffn",
    )(*args)


def _proj_kernel(h_ref, g_ref, w_ref, gb_ref, qa_ref, ka_ref, va_ref, qb_ref, kb_ref, vb_ref,
                 c_ref, gate_ref):
    u = _rms(h_ref[...], g_ref[...]).astype(BF16)

    def seg(off, width):
        return jnp.dot(u, w_ref[:, off:off + width], preferred_element_type=F32)

    qa_ref[...] = (seg(OFF_SBQ, ATT_W) * QK_SCALE).astype(BF16)
    ka_ref[...] = seg(OFF_SBK, ATT_W).astype(BF16)
    va_ref[...] = seg(OFF_SBV, ATT_W).astype(BF16)
    qb_ref[...] = (seg(OFF_MBQ, ATT_W) * QK_SCALE).astype(BF16)
    kb_ref[...] = seg(OFF_MBK, ATT_W).astype(BF16)
    vb_ref[...] = seg(OFF_MBV, ATT_W).astype(BF16)
    c_ref[...] = seg(OFF_CONV, CONV_CH) * jax.nn.sigmoid(seg(OFF_CONV + CONV_CH, CONV_CH))
    for j in range(N_BRANCH):
        lo = j * D_MODEL
        gate_ref[:, lo:lo + D_MODEL] = jax.nn.sigmoid(
            seg(OFF_GATE + lo, D_MODEL) + gb_ref[:, lo:lo + D_MODEL])


def _proj(h, g, w, gb):
    n = h.shape[0]

    def row(width):
        return pl.BlockSpec((ROW_TILE, width), lambda i: (i, 0))

    att = jax.ShapeDtypeStruct((n, ATT_W), BF16)
    return pl.pallas_call(
        _proj_kernel,
        grid=(n // ROW_TILE,),
        in_specs=[row(D_MODEL), _resident((1, D_MODEL)), _resident((D_MODEL, IN_COLS)),
                  _resident((1, N_BRANCH * D_MODEL))],
        out_specs=[row(ATT_W)] * 6 + [row(CONV_CH), row(N_BRANCH * D_MODEL)],
        out_shape=[att] * 6 + [jax.ShapeDtypeStruct((n, CONV_CH), F32),
                               jax.ShapeDtypeStruct((n, N_BRANCH * D_MODEL), F32)],
        compiler_params=_params(("parallel",)),
        name="proj",
    )(h, g, w, gb)


def _head_lanes(shape, hh):
    lane = lax.broadcasted_iota(jnp.int32, shape, len(shape) - 1)
    return (lane >= hh * HEAD_DIM) & (lane < (hh + 1) * HEAD_DIM)


def _dot_nt(a, b):
    return lax.dot_general(a, b, (((1,), (1,)), ((), ())), preferred_element_type=F32)


def _split_bf16(x):
    hi = x.astype(BF16)
    return hi, (x - hi.astype(F32)).astype(BF16)


def _att_specs(seq):
    blk = pl.BlockSpec((1, seq, LANES), lambda b, p: (b, 0, p))
    return blk


def _sb_kernel(q_ref, k_ref, v_ref, o_ref, *, seq):
    t = ATT_TILE
    n_blk = seq // t
    r = lax.broadcasted_iota(jnp.int32, (t, t), 0)
    c = lax.broadcasted_iota(jnp.int32, (t, t), 1)
    suffix = (r >= c).astype(BF16)
    past = c < r

    def q_block(qb, _):
        q0 = pl.multiple_of(qb * t, t)
        q_pair = q_ref[0, pl.ds(q0, t), :]
        outs = []
        for hh in range(PAIR):
            q = jnp.where(_head_lanes((t, LANES), hh), q_pair, jnp.zeros_like(q_pair))

            def tile(kb, carry, acc, diag, q=q):
                k0 = pl.multiple_of(kb * t, t)
                z = _dot_nt(q, k_ref[0, pl.ds(k0, t), :])
                sp = jnp.log(1.0 + jnp.exp(-jnp.abs(z)))
                mn = jnp.minimum(z, 0.0)
                log_beta = mn - sp
                log_1m = (mn - z) - sp
                if diag:
                    log_1m = jnp.where(past, log_1m, 0.0)
                hi, lo = _split_bf16(log_1m)
                incl = (jnp.dot(hi, suffix, preferred_element_type=F32)
                        + jnp.dot(lo, suffix, preferred_element_type=F32))
                w = jnp.exp(log_beta + (incl - log_1m) + carry)
                if diag:
                    w = jnp.where(past, w, 0.0)
                acc = acc + jnp.dot(w.astype(BF16), v_ref[0, pl.ds(k0, t), :],
                                    preferred_element_type=F32)
                return carry + incl[:, 0:1], acc

            carry, acc = tile(qb, jnp.zeros((t, 1), F32), jnp.zeros((t, LANES), F32), True)
            carry, acc = lax.fori_loop(
                0, qb, lambda i, ca: tile(qb - 1 - i, ca[0], ca[1], False), (carry, acc))
            outs.append(acc)
        o = jnp.where(_head_lanes((t, LANES), 0), outs[0], outs[1])
        o_ref[0, pl.ds(q0, t), :] = o.astype(o_ref.dtype)
        return 0

    lax.fori_loop(0, n_blk, q_block, 0)


def _sb_attention(q, k, v):
    b, seq, _ = q.shape
    blk = _att_specs(seq)
    return pl.pallas_call(
        functools.partial(_sb_kernel, seq=seq),
        grid=(b, N_HEADS // PAIR),
        in_specs=[blk, blk, blk],
        out_specs=blk,
        out_shape=jax.ShapeDtypeStruct((b, seq, ATT_W), BF16),
        compiler_params=_params(("parallel", "parallel")),
        name="stickbreak",
    )(q, k, v)


def _moba_kernel(slope_ref, q_ref, k_ref, v_ref, o_ref, qaug_ref, *, seq):
    t = ATT_TILE
    n_blk = seq // t
    pair = pl.program_id(1)

    kmean = jnp.concatenate(
        [jnp.mean(k_ref[0, j * t:(j + 1) * t, :].astype(F32), axis=0, keepdims=True)
         for j in range(n_blk)], axis=0)

    row_blk = lax.broadcasted_iota(jnp.int32, (seq, LANES), 0) // t
    lane = lax.broadcasted_iota(jnp.int32, (seq, LANES), 1)
    cand = lane % n_blk
    q_all = q_ref[0]
    for hh in range(PAIR):
        head = _head_lanes((seq, LANES), hh)
        km_t = jnp.where(_head_lanes((n_blk, LANES), hh), kmean, 0.0)
        km_rep = jnp.concatenate([km_t] * (LANES // n_blk), axis=0)
        hi, lo = _split_bf16(km_rep)
        q_h = jnp.where(head, q_all, jnp.zeros_like(q_all))
        gate = _dot_nt(q_h, hi) + _dot_nt(q_h, lo)
        valid = cand < row_blk
        gate = jnp.where(valid, gate, -jnp.inf)
        rank = jnp.zeros((seq, LANES), jnp.int32)
        for s in range(1, n_blk):
            other = pltpu.roll(gate, s, axis=1)
            other_idx = (cand + (n_blk - s)) % n_blk
            ahead = (other > gate) | ((other == gate) & (other_idx < cand))
            rank = rank + ahead.astype(jnp.int32)
        unsel = jnp.where(valid & (rank < MOBA_TOPK), 0.0, 1.0).astype(BF16)
        qaug_ref[hh] = jnp.where(head, q_all, unsel)

    row = lax.broadcasted_iota(jnp.int32, (t, t), 0)
    col = lax.broadcasted_iota(jnp.int32, (t, t), 1)
    causal = col <= row
    dist = (row - col).astype(F32)
    klane = lax.broadcasted_iota(jnp.int32, (t, LANES), 1)

    def q_block(qb, _):
        q0 = pl.multiple_of(qb * t, t)
        outs = []
        for hh in range(PAIR):
            slope = slope_ref[pair * PAIR + hh]
            bias = -slope * dist
            q = qaug_ref[hh, pl.ds(q0, t), :]
            khead = _head_lanes((t, LANES), hh)
            flag_lane0 = (1 - hh) * HEAD_DIM

            def tile(kb, m, l, acc, diag, q=q, bias=bias, khead=khead, slope=slope,
                     flag_lane0=flag_lane0):
                k0 = pl.multiple_of(kb * t, t)
                k_pair = k_ref[0, pl.ds(k0, t), :]
                if diag:
                    fill = jnp.zeros_like(k_pair)
                else:
                    fill = jnp.where(klane == flag_lane0 + kb, -MASK_BIG, 0.0).astype(BF16)
                s = _dot_nt(q, jnp.where(khead, k_pair, fill)) + bias
                if diag:
                    s = jnp.where(causal, s, NEG)
                    shift = 0.0
                else:
                    shift = slope * ((qb - kb) * t).astype(F32)
                m_new = jnp.maximum(m, jnp.max(s, axis=-1, keepdims=True) - shift)
                alpha = jnp.exp(m - m_new)
                p = jnp.exp(s - (m_new + shift))
                l = alpha * l + jnp.sum(p, axis=-1, keepdims=True)
                acc = alpha * acc + jnp.dot(p.astype(BF16), v_ref[0, pl.ds(k0, t), :],
                                            preferred_element_type=F32)
                return m_new, l, acc

            m, l, acc = tile(qb, jnp.full((t, 1), NEG, F32), jnp.zeros((t, 1), F32),
                             jnp.zeros((t, LANES), F32), True)
            m, l, acc = lax.fori_loop(
                0, qb, lambda i, c: tile(qb - 1 - i, c[0], c[1], c[2], False), (m, l, acc))
            outs.append(acc / l)
        o = jnp.where(_head_lanes((t, LANES), 0), outs[0], outs[1])
        o_ref[0, pl.ds(q0, t), :] = o.astype(o_ref.dtype)
        return 0

    lax.fori_loop(0, n_blk, q_block, 0)


def _moba_attention(q, k, v, slopes):
    b, seq, _ = q.shape
    assert seq % MOBA_BLOCK == 0 and ATT_TILE == MOBA_BLOCK and LANES % (seq // MOBA_BLOCK) == 0
    assert seq // MOBA_BLOCK <= HEAD_DIM
    blk = _att_specs(seq)
    return pl.pallas_call(
        functools.partial(_moba_kernel, seq=seq),
        grid=(b, N_HEADS // PAIR),
        in_specs=[pl.BlockSpec(memory_space=pltpu.SMEM), blk, blk, blk],
        out_specs=blk,
        out_shape=jax.ShapeDtypeStruct((b, seq, ATT_W), BF16),
        scratch_shapes=[pltpu.VMEM((PAIR, seq, LANES), BF16)],
        compiler_params=_params(("parallel", "parallel")),
        name="moba",
    )(slopes, q, k, v)


def _conv_kernel(c_ref, w_ref, b_ref, g_ref, beta_ref, o_ref, pad_ref, *, seq):
    pad_ref[0:CONV_HALO, :] = jnp.zeros((CONV_HALO, CONV_CH), F32)
    pad_ref[CONV_HALO:, :] = c_ref[0]
    rows = CONV_TILE + CONV_HALO
    lead = CONV_HALO - (CONV_WIDTH - 1)

    def step(i, _):
        r0 = pl.multiple_of(i * CONV_TILE, CONV_TILE)
        x = pad_ref[pl.ds(r0, rows), :]
        acc = jnp.zeros((CONV_TILE, CONV_CH), F32) + b_ref[...]
        for sub in range(8):
            xs = x if sub == 0 else pltpu.roll(x, rows - sub, axis=0)
            for tap in range(CONV_WIDTH):
                off = tap + lead
                if off % 8 == sub:
                    base = off - sub
                    acc = acc + xs[base:base + CONV_TILE, :] * w_ref[tap:tap + 1, :]
        mu = jnp.mean(acc, axis=-1, keepdims=True)
        d = acc - mu
        var = jnp.mean(d * d, axis=-1, keepdims=True)
        y = d * lax.rsqrt(var + EPS) * g_ref[...] + beta_ref[...]
        o_ref[0, pl.ds(r0, CONV_TILE), :] = (y * jax.nn.sigmoid(y)).astype(o_ref.dtype)
        return 0

    lax.fori_loop(0, seq // CONV_TILE, step, 0)


def _conv(c, w, b, g, beta):
    bsz, seq, _ = c.shape
    blk = pl.BlockSpec((1, seq, CONV_CH), lambda i: (i, 0, 0))
    return pl.pallas_call(
        functools.partial(_conv_kernel, seq=seq),
        grid=(bsz,),
        in_specs=[blk, _resident((CONV_WIDTH, CONV_CH)), _resident((1, CONV_CH)),
                  _resident((1, CONV_CH)), _resident((1, CONV_CH))],
        out_specs=blk,
        out_shape=jax.ShapeDtypeStruct((bsz, seq, CONV_CH), BF16),
        scratch_shapes=[pltpu.VMEM((seq + CONV_HALO, CONV_CH), F32)],
        compiler_params=_params(("parallel",)),
        name="conv",
    )(c, w, b, g, beta)


def _mix_kernel(h_ref, oa_ref, ob_ref, oc_ref, gate_ref, wa_ref, wb_ref, wc_ref, wo_ref, o_ref):
    mixed = None
    for j, (y_ref, w_ref) in enumerate(((oa_ref, wa_ref), (ob_ref, wb_ref), (oc_ref, wc_ref))):
        y = jnp.dot(y_ref[...], w_ref[...], preferred_element_type=F32)
        term = gate_ref[:, j * D_MODEL:(j + 1) * D_MODEL] * y
        mixed = term if mixed is None else mixed + term
    o_ref[...] = h_ref[...] + jnp.dot(mixed.astype(BF16), wo_ref[...], preferred_element_type=F32)


def _mix(h, oa, ob, oc, gate, wa, wb, wc, wo):
    n = h.shape[0]

    def row(width):
        return pl.BlockSpec((ROW_TILE, width), lambda i: (i, 0))

    return pl.pallas_call(
        _mix_kernel,
        grid=(n // ROW_TILE,),
        in_specs=[row(D_MODEL), row(ATT_W), row(ATT_W), row(CONV_CH), row(N_BRANCH * D_MODEL),
                  _resident((ATT_W, D_MODEL)), _resident((ATT_W, D_MODEL)),
                  _resident((CONV_CH, D_MODEL)), _resident((D_MODEL, D_MODEL))],
        out_specs=row(D_MODEL),
        out_shape=jax.ShapeDtypeStruct((n, D_MODEL), F32),
        compiler_params=_params(("parallel",)),
        name="mix",
    )(h, oa, ob, oc, gate, wa, wb, wc, wo)


def kernel(x, ffn1_norm, ffn1_w_in, ffn1_w_out, mix_norm, w_in, gate_bias, sb_w_out, mb_w_out,
           conv_dw, conv_dw_bias, conv_ln_g, conv_ln_b, conv_w_out, w_o, ffn2_norm, ffn2_w_in,
           ffn2_w_out, final_norm):
    bsz, seq, _ = x.shape
    n = bsz * seq
    depth = w_in.shape[0]
    assert n % ROW_TILE == 0 and seq % ATT_TILE == 0 and seq % CONV_TILE == 0
    slopes = jnp.exp2(-8.0 * jnp.arange(1, N_HEADS + 1, dtype=F32) / N_HEADS)

    def vec(a):
        return a.reshape(1, -1)

    def seq3(a):
        return a.reshape(bsz, seq, a.shape[-1])

    h = x.reshape(n, D_MODEL)
    for l in range(depth):
        h = _ffn(h, vec(ffn1_norm[l]), ffn1_w_in[l].astype(BF16), ffn1_w_out[l].astype(BF16))
        qa, ka, va, qb, kb, vb, c, gate = _proj(h, vec(mix_norm[l]), w_in[l].astype(BF16),
                                                vec(gate_bias[l]))
        oa = _sb_attention(seq3(qa), seq3(ka), seq3(va))
        ob = _moba_attention(seq3(qb), seq3(kb), seq3(vb), slopes)
        oc = _conv(seq3(c), conv_dw[l].reshape(CONV_WIDTH, CONV_CH), vec(conv_dw_bias[l]),
                   vec(conv_ln_g[l]), vec(conv_ln_b[l]))
        h = _mix(h, oa.reshape(n, ATT_W), ob.reshape(n, ATT_W), oc.reshape(n, CONV_CH), gate,
                 sb_w_out[l].astype(BF16), mb_w_out[l].astype(BF16), conv_w_out[l].astype(BF16),
                 w_o[l].astype(BF16))
        last = l == depth - 1
        h = _ffn(h, vec(ffn2_norm[l]), ffn2_w_in[l].astype(BF16), ffn2_w_out[l].astype(BF16),
                 vec(final_norm) if last else None)
    return h.reshape(bsz, seq, D_MODEL)
```

```python
import functools

import jax
import jax.numpy as jnp
from jax import lax
from jax.experimental import pallas as pl
from jax.experimental.pallas import tpu as pltpu

D_MODEL = 1024
HEAD_DIM = 64
N_HEADS = 8
ATT_W = N_HEADS * HEAD_DIM
CONV_CH = D_MODEL // 2
CONV_WIDTH = 31
MOBA_BLOCK = 256
MOBA_TOPK = 3
D_FF = 2816
N_BRANCH = 3
EPS = 1e-6
QK_SCALE = HEAD_DIM ** -0.5

OFF_SBQ = 0
OFF_SBK = OFF_SBQ + ATT_W
OFF_SBV = OFF_SBK + ATT_W
OFF_MBQ = OFF_SBV + ATT_W
OFF_MBK = OFF_MBQ + ATT_W
OFF_MBV = OFF_MBK + ATT_W
OFF_CONV = OFF_MBV + ATT_W
OFF_GATE = OFF_CONV + 2 * CONV_CH
IN_COLS = OFF_GATE + N_BRANCH * D_MODEL

LANES = 128
SUBLANES_BF16 = 16
PAIR = LANES // HEAD_DIM
FF_CHUNK = 256
ROW_TILE = 512
ATT_TILE = 256
Q_HALVES = 2
SUFFIX_TILE = 128
N_PREFETCH = 2
N_DEFER = 2
CONV_TILE = 64
CONV_HALO = 32
VMEM_LIMIT = 56 * 1024 * 1024
SATURATED = 128.0
MASK_BIG = 2.0 ** 100
NEG = -1e30

F32 = jnp.float32
BF16 = jnp.bfloat16


def _rms(x, g):
    ms = jnp.mean(x * x, axis=-1, keepdims=True)
    return x * lax.rsqrt(ms + EPS) * g


def _resident(shape):
    nd = len(shape)
    return pl.BlockSpec(shape, lambda *_: (0,) * nd, pipeline_mode=pl.Buffered(1))


def _params(sem):
    return pltpu.CompilerParams(dimension_semantics=sem, vmem_limit_bytes=VMEM_LIMIT)


def _ffn_kernel(x_ref, g_ref, w1_ref, w2_ref, *rest, final):
    if final:
        gf_ref, o_ref, mid_ref = rest
    else:
        o_ref, mid_ref = rest
    x = x_ref[...]
    xn = _rms(x, g_ref[...]).astype(BF16)
    for c in range(D_FF // FF_CHUNK):
        lo = c * FF_CHUNK
        a = jnp.dot(xn, w1_ref[:, lo:lo + FF_CHUNK], preferred_element_type=F32)
        b = jnp.dot(xn, w1_ref[:, D_FF + lo:D_FF + lo + FF_CHUNK], preferred_element_type=F32)
        mid_ref[:, lo:lo + FF_CHUNK] = (a * jax.nn.sigmoid(a) * b).astype(BF16)
    y = x + 0.5 * jnp.dot(mid_ref[...], w2_ref[...], preferred_element_type=F32)
    if final:
        y = _rms(y, gf_ref[...])
    o_ref[...] = y


def _ffn(h, g, w1, w2, gf=None):
    n = h.shape[0]
    final = gf is not None
    row = pl.BlockSpec((ROW_TILE, D_MODEL), lambda i: (i, 0))
    in_specs = [row, _resident((1, D_MODEL)), _resident((D_MODEL, 2 * D_FF)),
                _resident((D_FF, D_MODEL))]
    args = [h, g, w1, w2]
    if final:
        in_specs.append(_resident((1, D_MODEL)))
        args.append(gf)
    return pl.pallas_call(
        functools.partial(_ffn_kernel, final=final),
        grid=(n // ROW_TILE,),
        in_specs=in_specs,
        out_specs=row,
        out_shape=jax.ShapeDtypeStruct((n, D_MODEL), F32),
        scratch_shapes=[pltpu.VMEM((ROW_TILE, D_FF), BF16)],
        compiler_params=_params(("parallel",)),
        name="ffn_final" if final else "ffn",
    )(*args)


def _proj_kernel(h_ref, g_ref, w_ref, gb_ref, wqa_ref, wva_ref, wqb_ref, wvb_ref,
                 qa_ref, ka_ref, va_ref, qb_ref, kb_ref, vb_ref, c_ref, gate_ref):
    u = _rms(h_ref[...], g_ref[...]).astype(BF16)

    def seg(off, width):
        return jnp.dot(u, w_ref[:, off:off + width], preferred_element_type=F32)

    def seg_t(wt_ref, o_ref, scale):
        r = lax.dot_general(wt_ref[...], u, (((1,), (1,)), ((), ())), preferred_element_type=F32)
        r = (r * scale).astype(BF16)
        for j in range(ROW_TILE // ATT_TILE):
            o_ref[0, j] = r[:, j * ATT_TILE:(j + 1) * ATT_TILE]

    seg_t(wqa_ref, qa_ref, QK_SCALE)
    seg_t(wva_ref, va_ref, 1.0)
    seg_t(wqb_ref, qb_ref, QK_SCALE)
    seg_t(wvb_ref, vb_ref, 1.0)
    ka_ref[...] = seg(OFF_SBK, ATT_W).astype(BF16)
    kb_ref[...] = seg(OFF_MBK, ATT_W).astype(BF16)
    c_ref[...] = seg(OFF_CONV, CONV_CH) * jax.nn.sigmoid(seg(OFF_CONV + CONV_CH, CONV_CH))
    for j in range(N_BRANCH):
        lo = j * D_MODEL
        gate_ref[:, lo:lo + D_MODEL] = jax.nn.sigmoid(
            seg(OFF_GATE + lo, D_MODEL) + gb_ref[:, lo:lo + D_MODEL])


def _proj(h, g, w, gb, wt_qa, wt_va, wt_qb, wt_vb, bsz, seq):
    n = h.shape[0]
    tiles_per_seq = seq // ROW_TILE
    sub = ROW_TILE // ATT_TILE

    def row(width):
        return pl.BlockSpec((ROW_TILE, width), lambda i: (i, 0))

    feat = pl.BlockSpec((1, sub, ATT_W, ATT_TILE),
                        lambda i: (i // tiles_per_seq, i % tiles_per_seq, 0, 0))
    tok_shape = jax.ShapeDtypeStruct((n, ATT_W), BF16)
    feat_shape = jax.ShapeDtypeStruct((bsz, seq // ATT_TILE, ATT_W, ATT_TILE), BF16)
    wt = _resident((ATT_W, D_MODEL))
    return pl.pallas_call(
        _proj_kernel,
        grid=(n // ROW_TILE,),
        in_specs=[row(D_MODEL), _resident((1, D_MODEL)), _resident((D_MODEL, IN_COLS)),
                  _resident((1, N_BRANCH * D_MODEL)), wt, wt, wt, wt],
        out_specs=[feat, row(ATT_W), feat, feat, row(ATT_W), feat, row(CONV_CH),
                   row(N_BRANCH * D_MODEL)],
        out_shape=[feat_shape, tok_shape, feat_shape, feat_shape, tok_shape, feat_shape,
                   jax.ShapeDtypeStruct((n, CONV_CH), F32),
                   jax.ShapeDtypeStruct((n, N_BRANCH * D_MODEL), F32)],
        compiler_params=_params(("parallel",)),
        name="proj",
    )(h, g, w, gb, wt_qa, wt_va, wt_qb, wt_vb)


def _head_lanes(shape, hh):
    lane = lax.broadcasted_iota(jnp.int32, shape, len(shape) - 1)
    return (lane >= hh * HEAD_DIM) & (lane < (hh + 1) * HEAD_DIM)


def _head_rows(shape, hh):
    row = lax.broadcasted_iota(jnp.int32, shape, 0)
    return (row >= hh * HEAD_DIM) & (row < (hh + 1) * HEAD_DIM)


def _split_bf16(x):
    hi = x.astype(BF16)
    return hi, (x - hi.astype(F32)).astype(BF16)


def _mm(a, b):
    return jnp.dot(a, b, preferred_element_type=F32)


def _att_call(body, name, seq, bsz, extra_in=(), extra_specs=(), scratch=()):
    n_blk = seq // ATT_TILE
    feat = pl.BlockSpec((1, n_blk, LANES, ATT_TILE), lambda b, p: (b, 0, p, 0))
    tok = pl.BlockSpec((1, seq, LANES), lambda b, p: (b, 0, p))
    return pl.pallas_call(
        functools.partial(body, seq=seq),
        grid=(bsz, N_HEADS // PAIR),
        in_specs=list(extra_specs) + [feat, tok, feat],
        out_specs=tok,
        out_shape=jax.ShapeDtypeStruct((bsz, seq, ATT_W), BF16),
        scratch_shapes=list(scratch),
        compiler_params=_params(("parallel", "parallel")),
        name=name,
    )


def _att_scratch():
    t = ATT_TILE
    return [pltpu.VMEM((Q_HALVES * PAIR, LANES, t), F32),
            pltpu.VMEM((max(N_PREFETCH, 1), t, t), F32),
            pltpu.VMEM((max(N_DEFER, 1), t, t), BF16)]


def _store_pair(o_ref, qb, out_t):
    t = ATT_TILE
    o_t = jnp.where(_head_rows((LANES, t), 0), out_t[0], out_t[1])
    q0 = pl.multiple_of(qb * t, t)
    o_ref[0, pl.ds(q0, t), :] = o_t.T.astype(o_ref.dtype)


def _sb_kernel(qt_ref, k_ref, vt_ref, o_ref, acc_ref, zpre_ref, wlate_ref, *, seq):
    t = ATT_TILE
    n_sup = seq // (t * Q_HALVES)
    past = (lax.broadcasted_iota(jnp.int32, (t, t), 0)
            < lax.broadcasted_iota(jnp.int32, (t, t), 1))
    r = lax.broadcasted_iota(jnp.int32, (SUFFIX_TILE, SUFFIX_TILE), 0)
    c = lax.broadcasted_iota(jnp.int32, (SUFFIX_TILE, SUFFIX_TILE), 1)
    upper = (r <= c).astype(BF16)
    upper2 = jnp.concatenate([upper, upper], axis=1)
    n_sub = t // SUFFIX_TILE

    def scores(qs, kb):
        k0 = pl.multiple_of(kb * t, t)
        k_t = k_ref[0, pl.ds(k0, t), :]
        return [_mm(k_t, q) for q in qs]

    def weights(zs, carries, diags):
        stacks = []
        for z, diag in zip(zs, diags):
            sp = jnp.maximum(z, 0.0) + jnp.log(1.0 + jnp.exp(-jnp.abs(z)))
            if diag:
                sp = jnp.where(past, sp, 0.0)
            hi, lo = _split_bf16(sp)
            stacks.append([jnp.concatenate([hi[s * SUFFIX_TILE:(s + 1) * SUFFIX_TILE],
                                            lo[s * SUFFIX_TILE:(s + 1) * SUFFIX_TILE]], axis=0)
                           for s in range(n_sub)])
        incls = [[_mm(upper2, st) for st in stack] for stack in stacks]
        ws, new_carries = [], []
        for z, incl, carry, diag in zip(zs, incls, carries, diags):
            parts = [None] * n_sub
            for s in reversed(range(n_sub)):
                parts[s] = jnp.exp(z[s * SUFFIX_TILE:(s + 1) * SUFFIX_TILE] - incl[s] - carry)
                carry = carry + incl[s][0:1, :]
            w = jnp.concatenate(parts, axis=0)
            ws.append((jnp.where(past, w, 0.0) if diag else w).astype(BF16))
            new_carries.append(carry)
        return ws, new_carries

    def values(ws, idx, kb, first=None):
        v_t = vt_ref[0, kb]
        for i, (n, w) in enumerate(zip(idx, ws)):
            pv = _mm(v_t, w)
            acc_ref[n] = pv if first is not None and first[i] else acc_ref[n] + pv

    chains = [(hh, half) for half in range(Q_HALVES) for hh in range(PAIR)]
    n_ch = len(chains)
    n_now = n_ch - N_DEFER
    now, late = list(range(n_now)), list(range(n_now, n_ch))

    def sup_block(sb, _):
        qbs = [sb * Q_HALVES + i for i in range(Q_HALVES)]
        q = {}
        for half in range(Q_HALVES):
            q_pair = qt_ref[0, qbs[half]]
            for hh in range(PAIR):
                q[hh, half] = jnp.where(_head_rows((LANES, t), hh), q_pair, jnp.zeros_like(q_pair))
        carry = {c: jnp.zeros((1, t), F32) for c in chains}
        for kk in reversed(range(Q_HALVES)):
            active = [c for c in chains if c[1] >= kk]
            diags = [c[1] == kk for c in active]
            ws, cs = weights(scores([q[c] for c in active], qbs[kk]),
                             [carry[c] for c in active], diags)
            carry.update(zip(active, cs))
            values(ws, [chains.index(c) for c in active], qbs[kk], first=diags)

        qs = [q[c] for c in chains]

        def prefetch(kb):
            for j, z in enumerate(scores(qs[:N_PREFETCH], jnp.maximum(kb, 0))):
                zpre_ref[j] = z

        prefetch(qbs[0] - 1)
        for j in range(N_DEFER):
            wlate_ref[j] = jnp.zeros((t, t), BF16)

        def unsaturated(carries):
            return (jnp.min(functools.reduce(jnp.minimum, carries)) < SATURATED).astype(jnp.int32)

        def body(st):
            i, carries = st[0], st[2:]
            kb = qbs[0] - 1 - i
            zs = [zpre_ref[j] for j in range(N_PREFETCH)] + scores(qs[N_PREFETCH:], kb)
            prefetch(kb - 1)
            values([wlate_ref[j] for j in range(N_DEFER)], late, kb + 1)
            ws, carries = weights(zs, carries, [False] * n_ch)
            values(ws[:n_now], now, kb)
            for j in range(N_DEFER):
                wlate_ref[j] = ws[n_now + j]
            return (i + 1, unsaturated(carries), *carries)

        carries = [carry[c] for c in chains]
        st = lax.while_loop(lambda st: (st[0] < qbs[0]) & (st[1] > 0), body,
                            (jnp.int32(0), unsaturated(carries), *carries))
        values([wlate_ref[j] for j in range(N_DEFER)], late, qbs[0] - st[0])
        for half in range(Q_HALVES):
            _store_pair(o_ref, qbs[half],
                        [acc_ref[chains.index((hh, half))] for hh in range(PAIR)])
        return 0

    lax.fori_loop(0, n_sup, sup_block, 0)


def _sb_attention(qt, k, vt):
    bsz, seq, _ = k.shape
    return _att_call(_sb_kernel, "stickbreak", seq, bsz, scratch=_att_scratch())(qt, k, vt)


def _moba_kernel(slope_ref, qt_ref, k_ref, vt_ref, o_ref, acc_ref, spre_ref, plate_ref, bias_ref,
                 *, seq):
    t = ATT_TILE
    n_blk = seq // t
    n_sup = n_blk // Q_HALVES
    pair = pl.program_id(1)
    km_rows = max(n_blk, SUBLANES_BF16)

    kmean = jnp.concatenate(
        [jnp.mean(k_ref[0, j * t:(j + 1) * t, :].astype(F32), axis=0, keepdims=True)
         for j in range(n_blk)]
        + ([jnp.zeros((km_rows - n_blk, LANES), F32)] if km_rows > n_blk else []), axis=0)

    row = lax.broadcasted_iota(jnp.int32, (t, t), 0)
    col = lax.broadcasted_iota(jnp.int32, (t, t), 1)
    causal = row <= col
    dist = (col - row).astype(F32)
    klane = lax.broadcasted_iota(jnp.int32, (t, LANES), 1)
    cand = lax.broadcasted_iota(jnp.int32, (n_blk, t), 0)
    slopes, km_split, flag_row0 = [], [], []
    for hh in range(PAIR):
        slope = slope_ref[pair * PAIR + hh]
        slopes.append(slope)
        bias_ref[hh] = -slope * dist
        km_split.append(_split_bf16(jnp.where(_head_lanes((km_rows, LANES), hh), kmean, 0.0)))
        flag_row0.append((1 - hh) * HEAD_DIM)

    def q_with_flags(hh, qb):
        q_pair = qt_ref[0, qb]
        hi, lo = km_split[hh]
        gate = (_mm(hi, q_pair) + _mm(lo, q_pair))[:n_blk]
        valid = cand < qb
        gate = jnp.where(valid, gate, -jnp.inf)
        rank = jnp.zeros((n_blk, t), jnp.int32)
        for j in range(n_blk):
            other = gate[j:j + 1, :]
            ahead = (other > gate) | ((other == gate) & (j < cand))
            rank = rank + ahead.astype(jnp.int32)
        unsel = jnp.where(valid & (rank < MOBA_TOPK), 0.0, 1.0)
        pieces = [unsel, jnp.zeros((LANES - flag_row0[hh] - n_blk, t), F32)]
        if flag_row0[hh]:
            pieces.insert(0, jnp.zeros((flag_row0[hh], t), F32))
        flags = jnp.concatenate(pieces, axis=0).astype(BF16)
        return jnp.where(_head_rows((LANES, t), hh), q_pair, flags)

    def k_masked(hh, kb, diag):
        k0 = pl.multiple_of(kb * t, t)
        k_pair = k_ref[0, pl.ds(k0, t), :]
        if diag:
            fill = jnp.zeros_like(k_pair)
        else:
            fill = jnp.where(klane == flag_row0[hh] + kb, -MASK_BIG, 0.0).astype(BF16)
        return jnp.where(_head_lanes((t, LANES), hh), k_pair, fill)

    def scores(chains, qs, kb, diags):
        k_m = {hd: k_masked(hd[0], kb, hd[1]) for hd in sorted({(c[0], d) for c, d in zip(chains, diags)})}
        return [_mm(k_m[c[0], d], q) for c, q, d in zip(chains, qs, diags)]

    def probs(chains, ss, qbs, kb, ms, ls, diags):
        ps, alphas, new_ms, new_ls = [], [], [], []
        for (hh, half), s, m, l, diag in zip(chains, ss, ms, ls, diags):
            s = s + bias_ref[hh]
            if diag:
                s = jnp.where(causal, s, NEG)
                shift = 0.0
            else:
                shift = slopes[hh] * ((qbs[half] - kb) * t).astype(F32)
            m_new = jnp.maximum(m, jnp.max(s, axis=0, keepdims=True) - shift)
            alpha = jnp.exp(m - m_new)
            p = jnp.exp(s - (m_new + shift))
            ps.append(p.astype(BF16))
            alphas.append(alpha)
            new_ms.append(m_new)
            new_ls.append(alpha * l + jnp.sum(p, axis=0, keepdims=True))
        return ps, alphas, new_ms, new_ls

    def values(ps, alphas, idx, kb, first=None):
        v_t = vt_ref[0, kb]
        for i, (n, p, alpha) in enumerate(zip(idx, ps, alphas)):
            pv = _mm(v_t, p)
            acc_ref[n] = pv if first is not None and first[i] else alpha * acc_ref[n] + pv

    chains = [(hh, half) for half in range(Q_HALVES) for hh in range(PAIR)]
    n_ch = len(chains)
    n_now = n_ch - N_DEFER
    now, late = list(range(n_now)), list(range(n_now, n_ch))
    pre, rest = chains[:N_PREFETCH], chains[N_PREFETCH:]

    def sup_block(sb, _):
        qbs = [sb * Q_HALVES + i for i in range(Q_HALVES)]
        q = {c: q_with_flags(c[0], qbs[c[1]]) for c in chains}
        m = {c: jnp.full((1, t), NEG, F32) for c in chains}
        l = {c: jnp.zeros((1, t), F32) for c in chains}
        for kk in reversed(range(Q_HALVES)):
            active = [c for c in chains if c[1] >= kk]
            diags = [c[1] == kk for c in active]
            ss = scores(active, [q[c] for c in active], qbs[kk], diags)
            ps, alphas, ms, ls = probs(active, ss, qbs, qbs[kk], [m[c] for c in active],
                                       [l[c] for c in active], diags)
            m.update(zip(active, ms))
            l.update(zip(active, ls))
            values(ps, alphas, [chains.index(c) for c in active], qbs[kk], first=diags)

        qs = [q[c] for c in chains]

        def prefetch(kb):
            ss = scores(pre, qs[:N_PREFETCH], jnp.maximum(kb, 0), [False] * len(pre))
            for j, s in enumerate(ss):
                spre_ref[j] = s

        prefetch(qbs[0] - 1)
        for j in range(N_DEFER):
            plate_ref[j] = jnp.zeros((t, t), BF16)

        def body(i, st):
            kb = qbs[0] - 1 - i
            ms, ls = list(st[:n_ch]), list(st[n_ch:2 * n_ch])
            alphas_late = list(st[2 * n_ch:])
            ss = ([spre_ref[j] for j in range(N_PREFETCH)]
                  + scores(rest, qs[N_PREFETCH:], kb, [False] * len(rest)))
            prefetch(kb - 1)
            values([plate_ref[j] for j in range(N_DEFER)], alphas_late, late, kb + 1)
            ps, alphas, ms, ls = probs(chains, ss, qbs, kb, ms, ls, [False] * n_ch)
            values(ps[:n_now], alphas[:n_now], now, kb)
            for j in range(N_DEFER):
                plate_ref[j] = ps[n_now + j]
            return (*ms, *ls, *alphas[n_now:])

        st = lax.fori_loop(
            0, qbs[0], body,
            (*[m[c] for c in chains], *[l[c] for c in chains], *[jnp.ones((1, t), F32)] * N_DEFER))
        ls = st[n_ch:2 * n_ch]
        values([plate_ref[j] for j in range(N_DEFER)], st[2 * n_ch:], late, 0)
        for half in range(Q_HALVES):
            outs = [acc_ref[n] / ls[n] for n in (chains.index((hh, half)) for hh in range(PAIR))]
            _store_pair(o_ref, qbs[half], outs)
        return 0

    lax.fori_loop(0, n_sup, sup_block, 0)


def _moba_attention(qt, k, vt, slopes):
    bsz, seq, _ = k.shape
    n_blk = seq // MOBA_BLOCK
    assert ATT_TILE == MOBA_BLOCK and HEAD_DIM % n_blk == 0
    call = _att_call(_moba_kernel, "moba", seq, bsz,
                     extra_specs=[pl.BlockSpec(memory_space=pltpu.SMEM)],
                     scratch=_att_scratch() + [pltpu.VMEM((PAIR, ATT_TILE, ATT_TILE), F32)])
    return call(slopes, qt, k, vt)


def _conv_kernel(c_ref, w_ref, b_ref, g_ref, beta_ref, o_ref, pad_ref, *, seq):
    pad_ref[0:CONV_HALO, :] = jnp.zeros((CONV_HALO, CONV_CH), F32)
    pad_ref[CONV_HALO:, :] = c_ref[0]
    rows = CONV_TILE + CONV_HALO
    lead = CONV_HALO - (CONV_WIDTH - 1)

    def step(i, _):
        r0 = pl.multiple_of(i * CONV_TILE, CONV_TILE)
        x = pad_ref[pl.ds(r0, rows), :]
        acc = jnp.zeros((CONV_TILE, CONV_CH), F32) + b_ref[...]
        for sub in range(8):
            xs = x if sub == 0 else pltpu.roll(x, rows - sub, axis=0)
            for tap in range(CONV_WIDTH):
                off = tap + lead
                if off % 8 == sub:
                    base = off - sub
                    acc = acc + xs[base:base + CONV_TILE, :] * w_ref[tap:tap + 1, :]
        mu = jnp.mean(acc, axis=-1, keepdims=True)
        d = acc - mu
        var = jnp.mean(d * d, axis=-1, keepdims=True)
        y = d * lax.rsqrt(var + EPS) * g_ref[...] + beta_ref[...]
        o_ref[0, pl.ds(r0, CONV_TILE), :] = (y * jax.nn.sigmoid(y)).astype(o_ref.dtype)
        return 0

    lax.fori_loop(0, seq // CONV_TILE, step, 0)


def _conv(c, w, b, g, beta):
    bsz, seq, _ = c.shape
    blk = pl.BlockSpec((1, seq, CONV_CH), lambda i: (i, 0, 0))
    return pl.pallas_call(
        functools.partial(_conv_kernel, seq=seq),
        grid=(bsz,),
        in_specs=[blk, _resident((CONV_WIDTH, CONV_CH)), _resident((1, CONV_CH)),
                  _resident((1, CONV_CH)), _resident((1, CONV_CH))],
        out_specs=blk,
        out_shape=jax.ShapeDtypeStruct((bsz, seq, CONV_CH), BF16),
        scratch_shapes=[pltpu.VMEM((seq + CONV_HALO, CONV_CH), F32)],
        compiler_params=_params(("parallel",)),
        name="conv",
    )(c, w, b, g, beta)


def _mix_kernel(h_ref, oa_ref, ob_ref, oc_ref, gate_ref, wa_ref, wb_ref, wc_ref, wo_ref, o_ref):
    mixed = None
    for j, (y_ref, w_ref) in enumerate(((oa_ref, wa_ref), (ob_ref, wb_ref), (oc_ref, wc_ref))):
        y = jnp.dot(y_ref[...], w_ref[...], preferred_element_type=F32)
        term = gate_ref[:, j * D_MODEL:(j + 1) * D_MODEL] * y
        mixed = term if mixed is None else mixed + term
    o_ref[...] = h_ref[...] + jnp.dot(mixed.astype(BF16), wo_ref[...], preferred_element_type=F32)


def _mix(h, oa, ob, oc, gate, wa, wb, wc, wo):
    n = h.shape[0]

    def row(width):
        return pl.BlockSpec((ROW_TILE, width), lambda i: (i, 0))

    return pl.pallas_call(
        _mix_kernel,
        grid=(n // ROW_TILE,),
        in_specs=[row(D_MODEL), row(ATT_W), row(ATT_W), row(CONV_CH), row(N_BRANCH * D_MODEL),
                  _resident((ATT_W, D_MODEL)), _resident((ATT_W, D_MODEL)),
                  _resident((CONV_CH, D_MODEL)), _resident((D_MODEL, D_MODEL))],
        out_specs=row(D_MODEL),
        out_shape=jax.ShapeDtypeStruct((n, D_MODEL), F32),
        compiler_params=_params(("parallel",)),
        name="mix",
    )(h, oa, ob, oc, gate, wa, wb, wc, wo)


def kernel(x, ffn1_norm, ffn1_w_in, ffn1_w_out, mix_norm, w_in, gate_bias, sb_w_out, mb_w_out,
           conv_dw, conv_dw_bias, conv_ln_g, conv_ln_b, conv_w_out, w_o, ffn2_norm, ffn2_w_in,
           ffn2_w_out, final_norm):
    bsz, seq, _ = x.shape
    n = bsz * seq
    depth = w_in.shape[0]
    assert seq % ROW_TILE == 0 and seq % (ATT_TILE * Q_HALVES) == 0 and seq % CONV_TILE == 0
    slopes = jnp.exp2(-8.0 * jnp.arange(1, N_HEADS + 1, dtype=F32) / N_HEADS)

    def vec(a):
        return a.reshape(1, -1)

    def seq3(a):
        return a.reshape(bsz, seq, a.shape[-1])

    h = x.reshape(n, D_MODEL)
    for l in range(depth):
        h = _ffn(h, vec(ffn1_norm[l]), ffn1_w_in[l].astype(BF16), ffn1_w_out[l].astype(BF16))
        w_l = w_in[l].astype(BF16)
        wt = [w_l[:, off:off + ATT_W].T for off in (OFF_SBQ, OFF_SBV, OFF_MBQ, OFF_MBV)]
        qa_t, ka, va_t, qb_t, kb, vb_t, c, gate = _proj(h, vec(mix_norm[l]), w_l,
                                                        vec(gate_bias[l]), *wt, bsz, seq)
        oa = _sb_attention(qa_t, seq3(ka), va_t)
        ob = _moba_attention(qb_t, seq3(kb), vb_t, slopes)
        oc = _conv(seq3(c), conv_dw[l].reshape(CONV_WIDTH, CONV_CH), vec(conv_dw_bias[l]),
                   vec(conv_ln_g[l]), vec(conv_ln_b[l]))
        h = _mix(h, oa.reshape(n, ATT_W), ob.reshape(n, ATT_W), oc.reshape(n, CONV_CH), gate,
                 sb_w_out[l].astype(BF16), mb_w_out[l].astype(BF16), conv_w_out[l].astype(BF16),
                 w_o[l].astype(BF16))
        last = l == depth - 1
        h = _ffn(h, vec(ffn2_norm[l]), ffn2_w_in[l].astype(BF16), ffn2_w_out[l].astype(BF16),
                 vec(final_norm) if last else None)
    return h.reshape(bsz, seq, D_MODEL)
```

```python
import functools
import math

import jax
import jax.numpy as jnp
import numpy as np
from jax import lax
from jax.experimental import pallas as pl
from jax.experimental.pallas import tpu as pltpu

D_MODEL = 1024
HEAD_DIM = 64
N_HEADS = 8
ATT_W = N_HEADS * HEAD_DIM
CONV_CH = D_MODEL // 2
CONV_WIDTH = 31
MOBA_BLOCK = 256
MOBA_TOPK = 3
D_FF = 2816
N_BRANCH = 3
EPS = 1e-6
QK_SCALE = HEAD_DIM ** -0.5

OFF_SBQ = 0
OFF_SBK = OFF_SBQ + ATT_W
OFF_SBV = OFF_SBK + ATT_W
OFF_MBQ = OFF_SBV + ATT_W
OFF_MBK = OFF_MBQ + ATT_W
OFF_MBV = OFF_MBK + ATT_W
OFF_CONV = OFF_MBV + ATT_W
OFF_GATE = OFF_CONV + 2 * CONV_CH
IN_COLS = OFF_GATE + N_BRANCH * D_MODEL

LANES = 128
SUBLANES_BF16 = 16
PAIR = LANES // HEAD_DIM
FF_CHUNK = 256
ROW_TILE = 512
ATT_TILE = 256
Q_HALVES = 2
PAIRS_PER_STEP = 2
SUFFIX_TILE = 128
N_PREFETCH = 2
N_DEFER = 2
CONV_TILE = 128
CONV_HALO = 32
VMEM_LIMIT = 56 * 1024 * 1024
SATURATED = 128.0
MASK_BIG = 2.0 ** 100
NEG = -1e30

F32 = jnp.float32
BF16 = jnp.bfloat16


def _rms(x, g):
    ms = jnp.mean(x * x, axis=-1, keepdims=True)
    return x * lax.rsqrt(ms + EPS) * g


def _resident(shape):
    nd = len(shape)
    return pl.BlockSpec(shape, lambda *_: (0,) * nd, pipeline_mode=pl.Buffered(1))


def _params(sem):
    return pltpu.CompilerParams(dimension_semantics=sem, vmem_limit_bytes=VMEM_LIMIT)


def _ffn_kernel(x_ref, g_ref, w1_ref, w2_ref, *rest, final):
    if final:
        gf_ref, o_ref, mid_ref = rest
    else:
        o_ref, mid_ref = rest
    x = x_ref[...]
    xn = _rms(x, g_ref[...]).astype(BF16)
    for c in range(D_FF // FF_CHUNK):
        lo = c * FF_CHUNK
        a = jnp.dot(xn, w1_ref[:, lo:lo + FF_CHUNK], preferred_element_type=F32)
        b = jnp.dot(xn, w1_ref[:, D_FF + lo:D_FF + lo + FF_CHUNK], preferred_element_type=F32)
        mid_ref[:, lo:lo + FF_CHUNK] = (a * jax.nn.sigmoid(a) * b).astype(BF16)
    y = x + 0.5 * jnp.dot(mid_ref[...], w2_ref[...], preferred_element_type=F32)
    if final:
        y = _rms(y, gf_ref[...])
    o_ref[...] = y


def _ffn(h, g, w1, w2, gf=None):
    n = h.shape[0]
    final = gf is not None
    row = pl.BlockSpec((ROW_TILE, D_MODEL), lambda i: (i, 0))
    in_specs = [row, _resident((1, D_MODEL)), _resident((D_MODEL, 2 * D_FF)),
                _resident((D_FF, D_MODEL))]
    args = [h, g, w1, w2]
    if final:
        in_specs.append(_resident((1, D_MODEL)))
        args.append(gf)
    return pl.pallas_call(
        functools.partial(_ffn_kernel, final=final),
        grid=(n // ROW_TILE,),
        in_specs=in_specs,
        out_specs=row,
        out_shape=jax.ShapeDtypeStruct((n, D_MODEL), F32),
        scratch_shapes=[pltpu.VMEM((ROW_TILE, D_FF), BF16)],
        compiler_params=_params(("parallel",)),
        name="ffn_final" if final else "ffn",
    )(*args)


def _proj_kernel(h_ref, g_ref, w_ref, gb_ref, wqa_ref, wva_ref, wqb_ref, wvb_ref,
                 qa_ref, ka_ref, va_ref, qb_ref, kb_ref, vb_ref, c_ref, gate_ref):
    u = _rms(h_ref[...], g_ref[...]).astype(BF16)

    def seg(off, width):
        return jnp.dot(u, w_ref[:, off:off + width], preferred_element_type=F32)

    def seg_t(wt_ref, o_ref, scale):
        r = lax.dot_general(wt_ref[...], u, (((1,), (1,)), ((), ())), preferred_element_type=F32)
        r = (r * scale).astype(BF16)
        for j in range(ROW_TILE // ATT_TILE):
            o_ref[0, j] = r[:, j * ATT_TILE:(j + 1) * ATT_TILE]

    seg_t(wqa_ref, qa_ref, QK_SCALE)
    seg_t(wva_ref, va_ref, 1.0)
    seg_t(wqb_ref, qb_ref, QK_SCALE)
    seg_t(wvb_ref, vb_ref, 1.0)
    ka_ref[...] = seg(OFF_SBK, ATT_W).astype(BF16)
    kb_ref[...] = seg(OFF_MBK, ATT_W).astype(BF16)
    c_ref[...] = seg(OFF_CONV, CONV_CH) * jax.nn.sigmoid(seg(OFF_CONV + CONV_CH, CONV_CH))
    for j in range(N_BRANCH):
        lo = j * D_MODEL
        gate_ref[:, lo:lo + D_MODEL] = jax.nn.sigmoid(
            seg(OFF_GATE + lo, D_MODEL) + gb_ref[:, lo:lo + D_MODEL])


def _proj(h, g, w, gb, wt_qa, wt_va, wt_qb, wt_vb, bsz, seq):
    n = h.shape[0]
    tiles_per_seq = seq // ROW_TILE
    sub = ROW_TILE // ATT_TILE

    def row(width):
        return pl.BlockSpec((ROW_TILE, width), lambda i: (i, 0))

    feat = pl.BlockSpec((1, sub, ATT_W, ATT_TILE),
                        lambda i: (i // tiles_per_seq, i % tiles_per_seq, 0, 0))
    tok_shape = jax.ShapeDtypeStruct((n, ATT_W), BF16)
    feat_shape = jax.ShapeDtypeStruct((bsz, seq // ATT_TILE, ATT_W, ATT_TILE), BF16)
    wt = _resident((ATT_W, D_MODEL))
    return pl.pallas_call(
        _proj_kernel,
        grid=(n // ROW_TILE,),
        in_specs=[row(D_MODEL), _resident((1, D_MODEL)), _resident((D_MODEL, IN_COLS)),
                  _resident((1, N_BRANCH * D_MODEL)), wt, wt, wt, wt],
        out_specs=[feat, row(ATT_W), feat, feat, row(ATT_W), feat, row(CONV_CH),
                   row(N_BRANCH * D_MODEL)],
        out_shape=[feat_shape, tok_shape, feat_shape, feat_shape, tok_shape, feat_shape,
                   jax.ShapeDtypeStruct((n, CONV_CH), F32),
                   jax.ShapeDtypeStruct((n, N_BRANCH * D_MODEL), F32)],
        compiler_params=_params(("parallel",)),
        name="proj",
    )(h, g, w, gb, wt_qa, wt_va, wt_qb, wt_vb)


def _head_lanes(shape, hh):
    lane = lax.broadcasted_iota(jnp.int32, shape, len(shape) - 1)
    return (lane >= hh * HEAD_DIM) & (lane < (hh + 1) * HEAD_DIM)


def _head_rows(shape, hh):
    row = lax.broadcasted_iota(jnp.int32, shape, 0)
    return (row >= hh * HEAD_DIM) & (row < (hh + 1) * HEAD_DIM)


def _split_bf16(x):
    hi = x.astype(BF16)
    return hi, (x - hi.astype(F32)).astype(BF16)


def _mm(a, b):
    return jnp.dot(a, b, preferred_element_type=F32)


def _att_call(body, name, seq, bsz, extra_in=(), extra_specs=(), scratch=()):
    n_blk = seq // ATT_TILE
    width = PAIRS_PER_STEP * LANES
    feat = pl.BlockSpec((1, n_blk, width, ATT_TILE), lambda b, p: (b, 0, p, 0))
    tok = pl.BlockSpec((1, seq, width), lambda b, p: (b, 0, p))
    return pl.pallas_call(
        functools.partial(body, seq=seq),
        grid=(bsz, N_HEADS // (PAIR * PAIRS_PER_STEP)),
        in_specs=list(extra_specs) + [feat, tok, feat],
        out_specs=tok,
        out_shape=jax.ShapeDtypeStruct((bsz, seq, ATT_W), BF16),
        scratch_shapes=list(scratch),
        compiler_params=_params(("parallel", "parallel")),
        name=name,
    )


def _acc_scratch():
    return pltpu.VMEM((len(_chains()), LANES, ATT_TILE), F32)


def _chains():
    return [(pp, hh, half) for half in range(Q_HALVES) for pp in range(PAIRS_PER_STEP)
            for hh in range(PAIR)]


def _pair_lanes(pp):
    return slice(pp * LANES, (pp + 1) * LANES)


def _store_pair(o_ref, qb, pp, out_t):
    t = ATT_TILE
    o_t = jnp.where(_head_rows((LANES, t), 0), out_t[0], out_t[1])
    q0 = pl.multiple_of(qb * t, t)
    o_ref[0, pl.ds(q0, t), _pair_lanes(pp)] = o_t.T.astype(o_ref.dtype)


def _sb_kernel(qt_ref, k_ref, vt_ref, o_ref, acc_ref, *, seq):
    t = ATT_TILE
    n_sup = seq // (t * Q_HALVES)
    past = (lax.broadcasted_iota(jnp.int32, (t, t), 0)
            < lax.broadcasted_iota(jnp.int32, (t, t), 1))
    r = lax.broadcasted_iota(jnp.int32, (SUFFIX_TILE, SUFFIX_TILE), 0)
    c = lax.broadcasted_iota(jnp.int32, (SUFFIX_TILE, SUFFIX_TILE), 1)
    upper = (r <= c).astype(BF16)
    upper2 = jnp.concatenate([upper, upper], axis=1)
    n_sub = t // SUFFIX_TILE

    def scores(cs, qs, kb):
        k0 = pl.multiple_of(kb * t, t)
        k_t = {pp: k_ref[0, pl.ds(k0, t), _pair_lanes(pp)] for pp in sorted({c[0] for c in cs})}
        return [_mm(k_t[c[0]], q) for c, q in zip(cs, qs)]

    def weights(zs, carries, diags):
        stacks = []
        for z, diag in zip(zs, diags):
            sp = jnp.maximum(z, 0.0) + jnp.log(1.0 + jnp.exp(-jnp.abs(z)))
            if diag:
                sp = jnp.where(past, sp, 0.0)
            hi, lo = _split_bf16(sp)
            stacks.append([jnp.concatenate([hi[s * SUFFIX_TILE:(s + 1) * SUFFIX_TILE],
                                            lo[s * SUFFIX_TILE:(s + 1) * SUFFIX_TILE]], axis=0)
                           for s in range(n_sub)])
        incls = [[_mm(upper2, st) for st in stack] for stack in stacks]
        ws, new_carries = [], []
        for z, incl, carry, diag in zip(zs, incls, carries, diags):
            parts = [None] * n_sub
            for s in reversed(range(n_sub)):
                parts[s] = jnp.exp(z[s * SUFFIX_TILE:(s + 1) * SUFFIX_TILE] - incl[s] - carry)
                carry = carry + incl[s][0:1, :]
            w = jnp.concatenate(parts, axis=0)
            ws.append((jnp.where(past, w, 0.0) if diag else w).astype(BF16))
            new_carries.append(carry)
        return ws, new_carries

    chains = _chains()

    def values(ws, idx, kb, first=None):
        for i, (n, w) in enumerate(zip(idx, ws)):
            pv = _mm(vt_ref[0, kb, _pair_lanes(chains[n][0]), :], w)
            acc_ref[n] = pv if first is not None and first[i] else acc_ref[n] + pv

    def sup_block(sb, _):
        qbs = [sb * Q_HALVES + i for i in range(Q_HALVES)]
        q = {}
        for pp, hh, half in chains:
            q_pair = qt_ref[0, qbs[half], _pair_lanes(pp), :]
            q[pp, hh, half] = jnp.where(_head_rows((LANES, t), hh), q_pair, jnp.zeros_like(q_pair))
        carry = {c: jnp.zeros((1, t), F32) for c in chains}
        for kk in reversed(range(Q_HALVES)):
            active = [c for c in chains if c[2] >= kk]
            diags = [c[2] == kk for c in active]
            ws, cs = weights(scores(active, [q[c] for c in active], qbs[kk]),
                             [carry[c] for c in active], diags)
            carry.update(zip(active, cs))
            values(ws, [chains.index(c) for c in active], qbs[kk], first=diags)

        def unsaturated(carries):
            return (jnp.min(functools.reduce(jnp.minimum, carries)) < SATURATED).astype(jnp.int32)

        qs = [q[c] for c in chains]

        def body(st):
            i, carries = st[0], st[2:]
            kb = qbs[0] - 1 - i
            ws, carries = weights(scores(chains, qs, kb), carries, [False] * len(chains))
            values(ws, range(len(chains)), kb)
            return (i + 1, unsaturated(carries), *carries)

        carries = [carry[c] for c in chains]
        lax.while_loop(lambda st: (st[0] < qbs[0]) & (st[1] > 0), body,
                       (jnp.int32(0), unsaturated(carries), *carries))
        for half in range(Q_HALVES):
            for pp in range(PAIRS_PER_STEP):
                _store_pair(o_ref, qbs[half], pp,
                            [acc_ref[chains.index((pp, hh, half))] for hh in range(PAIR)])
        return 0

    lax.fori_loop(0, n_sup, sup_block, 0)


def _sb_attention(qt, k, vt):
    bsz, seq, _ = k.shape
    return _att_call(_sb_kernel, "stickbreak", seq, bsz, scratch=[_acc_scratch()])(qt, k, vt)


def _moba_kernel(slope_ref, qt_ref, k_ref, vt_ref, o_ref, acc_ref, spre_ref, plate_ref, vaug_ref,
                 *, seq):
    t = ATT_TILE
    n_blk = seq // t
    n_sup = n_blk // Q_HALVES
    km_rows = max(n_blk, SUBLANES_BF16)
    causal = (lax.broadcasted_iota(jnp.int32, (t, t), 0)
              <= lax.broadcasted_iota(jnp.int32, (t, t), 1))
    klane = lax.broadcasted_iota(jnp.int32, (t, LANES), 1)
    krow = lax.broadcasted_iota(jnp.int32, (t, LANES), 0).astype(F32)
    qrow = lax.broadcasted_iota(jnp.int32, (LANES, t), 0)
    qcol = lax.broadcasted_iota(jnp.int32, (LANES, t), 1).astype(F32)
    cand = lax.broadcasted_iota(jnp.int32, (n_blk, t), 0)
    extra0 = [(1 - hh) * HEAD_DIM for hh in range(PAIR)]
    slopes, km_split, k_fill = {}, {}, {}
    for pp in range(PAIRS_PER_STEP):
        kmean = jnp.concatenate(
            [jnp.mean(k_ref[0, j * t:(j + 1) * t, _pair_lanes(pp)].astype(F32), axis=0,
                      keepdims=True) for j in range(n_blk)]
            + ([jnp.zeros((km_rows - n_blk, LANES), F32)] if km_rows > n_blk else []), axis=0)
        for hh in range(PAIR):
            slope = slope_ref[(pl.program_id(1) * PAIRS_PER_STEP + pp) * PAIR + hh]
            slopes[pp, hh] = slope
            km_split[pp, hh] = _split_bf16(
                jnp.where(_head_lanes((km_rows, LANES), hh), kmean, 0.0))
            x0 = extra0[hh]
            k_fill[pp, hh] = jnp.where(
                klane == x0 + n_blk, 1.0,
                jnp.where(klane == x0 + n_blk + 1, slope * krow, 0.0)).astype(BF16)
            for j in range(n_blk):
                vaug_ref[pp * PAIR + hh, j] = jnp.where(
                    qrow == x0, jnp.ones((), BF16), vt_ref[0, j, _pair_lanes(pp), :])

    def q_with_extras(pp, hh, qb):
        q_pair = qt_ref[0, qb, _pair_lanes(pp), :]
        hi, lo = km_split[pp, hh]
        gate = (_mm(hi, q_pair) + _mm(lo, q_pair))[:n_blk]
        valid = cand < qb
        gate = jnp.where(valid, gate, -jnp.inf)
        rank = jnp.zeros((n_blk, t), jnp.int32)
        for j in range(n_blk):
            other = gate[j:j + 1, :]
            ahead = (other > gate) | ((other == gate) & (j < cand))
            rank = rank + ahead.astype(jnp.int32)
        unsel = jnp.where(valid & (rank < MOBA_TOPK), 0.0, 1.0)
        x0 = extra0[hh]
        pieces = [unsel, jnp.zeros((LANES - x0 - n_blk, t), F32)]
        if x0:
            pieces.insert(0, jnp.zeros((x0, t), F32))
        extras = jnp.concatenate(pieces, axis=0)
        extras = jnp.where(qrow == x0 + n_blk, -slopes[pp, hh] * qcol,
                           jnp.where(qrow == x0 + n_blk + 1, 1.0, extras))
        return jnp.where(_head_rows((LANES, t), hh), q_pair, extras.astype(BF16))

    def k_with_extras(pp, hh, kb, diag):
        k0 = pl.multiple_of(kb * t, t)
        k_pair = k_ref[0, pl.ds(k0, t), _pair_lanes(pp)]
        fill = k_fill[pp, hh]
        if not diag:
            fill = jnp.where(klane == extra0[hh] + kb, jnp.full((), -MASK_BIG, BF16), fill)
        return jnp.where(_head_lanes((t, LANES), hh), k_pair, fill)

    def scores(chains, qs, kb, diags):
        k_x = {phd: k_with_extras(*phd[:2], kb, phd[2])
               for phd in sorted({(c[0], c[1], d) for c, d in zip(chains, diags)})}
        return [_mm(k_x[c[0], c[1], d], q) for c, q, d in zip(chains, qs, diags)]

    def probs(chains, ss, qbs, kb, ms, diags):
        ps, alphas, new_ms = [], [], []
        for (pp, hh, half), s, m, diag in zip(chains, ss, ms, diags):
            if diag:
                s = jnp.where(causal, s, NEG)
                shift = 0.0
            else:
                shift = slopes[pp, hh] * ((qbs[half] - kb) * t).astype(F32)
            m_new = jnp.maximum(m, jnp.max(s, axis=0, keepdims=True) - shift)
            alphas.append(jnp.exp(m - m_new))
            ps.append(jnp.exp(s - (m_new + shift)).astype(BF16))
            new_ms.append(m_new)
        return ps, alphas, new_ms

    chains = _chains()

    def values(ps, alphas, idx, kb, first=None):
        for i, (n, p, alpha) in enumerate(zip(idx, ps, alphas)):
            pv = _mm(vaug_ref[chains[n][0] * PAIR + chains[n][1], kb], p)
            acc_ref[n] = pv if first is not None and first[i] else alpha * acc_ref[n] + pv

    def sup_block(sb, _):
        qbs = [sb * Q_HALVES + i for i in range(Q_HALVES)]
        q = {c: q_with_extras(c[0], c[1], qbs[c[2]]) for c in chains}
        m = {c: jnp.full((1, t), NEG, F32) for c in chains}
        for kk in reversed(range(Q_HALVES)):
            active = [c for c in chains if c[2] >= kk]
            diags = [c[2] == kk for c in active]
            ss = scores(active, [q[c] for c in active], qbs[kk], diags)
            ps, alphas, ms = probs(active, ss, qbs, qbs[kk], [m[c] for c in active], diags)
            m.update(zip(active, ms))
            values(ps, alphas, [chains.index(c) for c in active], qbs[kk], first=diags)

        for pp in range(PAIRS_PER_STEP):
            sub = [c for c in chains if c[0] == pp]
            idx = [chains.index(c) for c in sub]
            qs = [q[c] for c in sub]
            n_sub = len(sub)
            n_now = n_sub - N_DEFER

            def prefetch(kb, sub=sub, qs=qs):
                ss = scores(sub[:N_PREFETCH], qs[:N_PREFETCH], jnp.maximum(kb, 0),
                            [False] * N_PREFETCH)
                for j, s in enumerate(ss):
                    spre_ref[j] = s

            prefetch(qbs[0] - 1)
            for j in range(N_DEFER):
                plate_ref[j] = jnp.zeros((t, t), BF16)

            def body(i, st, sub=sub, idx=idx, qs=qs, n_sub=n_sub, n_now=n_now, prefetch=prefetch):
                kb = qbs[0] - 1 - i
                ms, alphas_late = list(st[:n_sub]), list(st[n_sub:])
                ss = ([spre_ref[j] for j in range(N_PREFETCH)]
                      + scores(sub[N_PREFETCH:], qs[N_PREFETCH:], kb, [False] * (n_sub - N_PREFETCH)))
                prefetch(kb - 1)
                values([plate_ref[j] for j in range(N_DEFER)], alphas_late, idx[n_now:], kb + 1)
                ps, alphas, ms = probs(sub, ss, qbs, kb, ms, [False] * n_sub)
                values(ps[:n_now], alphas[:n_now], idx[:n_now], kb)
                for j in range(N_DEFER):
                    plate_ref[j] = ps[n_now + j]
                return (*ms, *alphas[n_now:])

            st = lax.fori_loop(0, qbs[0], body,
                               (*[m[c] for c in sub], *[jnp.ones((1, t), F32)] * N_DEFER))
            values([plate_ref[j] for j in range(N_DEFER)], st[n_sub:], idx[n_now:], 0)
        for half in range(Q_HALVES):
            for pp in range(PAIRS_PER_STEP):
                outs = []
                for hh in range(PAIR):
                    acc = acc_ref[chains.index((pp, hh, half))]
                    outs.append(acc / acc[extra0[hh]:extra0[hh] + 1, :])
                _store_pair(o_ref, qbs[half], pp, outs)
        return 0

    lax.fori_loop(0, n_sup, sup_block, 0)


def _moba_attention(qt, k, vt, slopes):
    bsz, seq, _ = k.shape
    n_blk = seq // MOBA_BLOCK
    assert ATT_TILE == MOBA_BLOCK and HEAD_DIM % n_blk == 0 and n_blk + 2 <= HEAD_DIM
    call = _att_call(_moba_kernel, "moba", seq, bsz,
                     extra_specs=[pl.BlockSpec(memory_space=pltpu.SMEM)],
                     scratch=[_acc_scratch(),
                              pltpu.VMEM((N_PREFETCH, ATT_TILE, ATT_TILE), F32),
                              pltpu.VMEM((N_DEFER, ATT_TILE, ATT_TILE), BF16),
                              pltpu.VMEM((PAIRS_PER_STEP * PAIR, n_blk, LANES, ATT_TILE), BF16)])
    return call(slopes, qt, k, vt)


def _conv_kernel(c_ref, w_ref, b_ref, g_ref, beta_ref, o_ref, pad_ref, *, seq):
    pad_ref[0:CONV_HALO, :] = jnp.zeros((CONV_HALO, CONV_CH), F32)
    pad_ref[CONV_HALO:, :] = c_ref[0]
    rows = CONV_TILE + CONV_HALO
    lead = CONV_HALO - (CONV_WIDTH - 1)

    def step(i, _):
        r0 = pl.multiple_of(i * CONV_TILE, CONV_TILE)
        chunks = []
        for ch in range(CONV_CH // LANES):
            lanes = slice(ch * LANES, (ch + 1) * LANES)
            x = pad_ref[pl.ds(r0, rows), lanes]
            acc = jnp.zeros((CONV_TILE, LANES), F32) + b_ref[:, lanes]
            for sub in range(8):
                xs = x if sub == 0 else pltpu.roll(x, rows - sub, axis=0)
                for tap in range(CONV_WIDTH):
                    off = tap + lead
                    if off % 8 == sub:
                        base = off - sub
                        acc = acc + xs[base:base + CONV_TILE, :] * w_ref[tap:tap + 1, lanes]
            chunks.append(acc)
        acc = jnp.concatenate(chunks, axis=1)
        mu = jnp.mean(acc, axis=-1, keepdims=True)
        d = acc - mu
        var = jnp.mean(d * d, axis=-1, keepdims=True)
        y = d * lax.rsqrt(var + EPS) * g_ref[...] + beta_ref[...]
        o_ref[0, pl.ds(r0, CONV_TILE), :] = (y * jax.nn.sigmoid(y)).astype(o_ref.dtype)
        return 0

    lax.fori_loop(0, seq // CONV_TILE, step, 0)


def _conv(c, w, b, g, beta):
    bsz, seq, _ = c.shape
    blk = pl.BlockSpec((1, seq, CONV_CH), lambda i: (i, 0, 0))
    return pl.pallas_call(
        functools.partial(_conv_kernel, seq=seq),
        grid=(bsz,),
        in_specs=[blk, _resident((CONV_WIDTH, CONV_CH)), _resident((1, CONV_CH)),
                  _resident((1, CONV_CH)), _resident((1, CONV_CH))],
        out_specs=blk,
        out_shape=jax.ShapeDtypeStruct((bsz, seq, CONV_CH), BF16),
        scratch_shapes=[pltpu.VMEM((seq + CONV_HALO, CONV_CH), F32)],
        compiler_params=_params(("parallel",)),
        name="conv",
    )(c, w, b, g, beta)


def _mix_kernel(h_ref, oa_ref, ob_ref, oc_ref, gate_ref, wa_ref, wb_ref, wc_ref, wo_ref, o_ref):
    mixed = None
    for j, (y_ref, w_ref) in enumerate(((oa_ref, wa_ref), (ob_ref, wb_ref), (oc_ref, wc_ref))):
        y = jnp.dot(y_ref[...], w_ref[...], preferred_element_type=F32)
        term = gate_ref[:, j * D_MODEL:(j + 1) * D_MODEL] * y
        mixed = term if mixed is None else mixed + term
    o_ref[...] = h_ref[...] + jnp.dot(mixed.astype(BF16), wo_ref[...], preferred_element_type=F32)


def _mix(h, oa, ob, oc, gate, wa, wb, wc, wo):
    n = h.shape[0]

    def row(width):
        return pl.BlockSpec((ROW_TILE, width), lambda i: (i, 0))

    return pl.pallas_call(
        _mix_kernel,
        grid=(n // ROW_TILE,),
        in_specs=[row(D_MODEL), row(ATT_W), row(ATT_W), row(CONV_CH), row(N_BRANCH * D_MODEL),
                  _resident((ATT_W, D_MODEL)), _resident((ATT_W, D_MODEL)),
                  _resident((CONV_CH, D_MODEL)), _resident((D_MODEL, D_MODEL))],
        out_specs=row(D_MODEL),
        out_shape=jax.ShapeDtypeStruct((n, D_MODEL), F32),
        compiler_params=_params(("parallel",)),
        name="mix",
    )(h, oa, ob, oc, gate, wa, wb, wc, wo)


def kernel(x, ffn1_norm, ffn1_w_in, ffn1_w_out, mix_norm, w_in, gate_bias, sb_w_out, mb_w_out,
           conv_dw, conv_dw_bias, conv_ln_g, conv_ln_b, conv_w_out, w_o, ffn2_norm, ffn2_w_in,
           ffn2_w_out, final_norm):
    bsz, seq, _ = x.shape
    n = bsz * seq
    depth = w_in.shape[0]
    assert seq % ROW_TILE == 0 and seq % (ATT_TILE * Q_HALVES) == 0 and seq % CONV_TILE == 0
    slopes_np = np.exp2(-8.0 * np.arange(1, N_HEADS + 1, dtype=np.float32) / N_HEADS)
    assert all(math.frexp(float(s))[0] == 0.5 for s in slopes_np) and ATT_TILE <= 256
    slopes = jnp.asarray(slopes_np)

    def vec(a):
        return a.reshape(1, -1)

    def seq3(a):
        return a.reshape(bsz, seq, a.shape[-1])

    h = x.reshape(n, D_MODEL)
    for l in range(depth):
        h = _ffn(h, vec(ffn1_norm[l]), ffn1_w_in[l].astype(BF16), ffn1_w_out[l].astype(BF16))
        w_l = w_in[l].astype(BF16)
        wt = [w_l[:, off:off + ATT_W].T for off in (OFF_SBQ, OFF_SBV, OFF_MBQ, OFF_MBV)]
        qa_t, ka, va_t, qb_t, kb, vb_t, c, gate = _proj(h, vec(mix_norm[l]), w_l,
                                                        vec(gate_bias[l]), *wt, bsz, seq)
        oa = _sb_attention(qa_t, seq3(ka), va_t)
        ob = _moba_attention(qb_t, seq3(kb), vb_t, slopes)
        oc = _conv(seq3(c), conv_dw[l].reshape(CONV_WIDTH, CONV_CH), vec(conv_dw_bias[l]),
                   vec(conv_ln_g[l]), vec(conv_ln_b[l]))
        h = _mix(h, oa.reshape(n, ATT_W), ob.reshape(n, ATT_W), oc.reshape(n, CONV_CH), gate,
                 sb_w_out[l].astype(BF16), mb_w_out[l].astype(BF16), conv_w_out[l].astype(BF16),
                 w_o[l].astype(BF16))
        last = l == depth - 1
        h = _ffn(h, vec(ffn2_norm[l]), ffn2_w_in[l].astype(BF16), ffn2_w_out[l].astype(BF16),
                 vec(final_norm) if last else None)
    return h.reshape(bsz, seq, D_MODEL)
```

```python
import functools
import math

import jax
import jax.numpy as jnp
import numpy as np
from jax import lax
from jax.experimental import pallas as pl
from jax.experimental.pallas import tpu as pltpu

D_MODEL = 1024
HEAD_DIM = 64
N_HEADS = 8
ATT_W = N_HEADS * HEAD_DIM
CONV_CH = D_MODEL // 2
CONV_WIDTH = 31
MOBA_BLOCK = 256
MOBA_TOPK = 3
D_FF = 2816
N_BRANCH = 3
EPS = 1e-6
QK_SCALE = HEAD_DIM ** -0.5

OFF_SBQ = 0
OFF_SBK = OFF_SBQ + ATT_W
OFF_SBV = OFF_SBK + ATT_W
OFF_MBQ = OFF_SBV + ATT_W
OFF_MBK = OFF_MBQ + ATT_W
OFF_MBV = OFF_MBK + ATT_W
OFF_CONV = OFF_MBV + ATT_W
OFF_GATE = OFF_CONV + 2 * CONV_CH
IN_COLS = OFF_GATE + N_BRANCH * D_MODEL

LANES = 128
SUBLANES_BF16 = 16
PAIR = LANES // HEAD_DIM
FF_CHUNK = 256
ROW_TILE = 512
ATT_TILE = 256
Q_HALVES = 2
PAIRS_PER_STEP = 2
SUFFIX_TILE = 128
N_PREFETCH = 2
N_DEFER = 2
CONV_TILE = 128
CONV_HALO = 32
VMEM_LIMIT = 56 * 1024 * 1024
SATURATED = 128.0
MASK_BIG = 2.0 ** 100
NEG = -1e30

F32 = jnp.float32
BF16 = jnp.bfloat16


def _rms(x, g):
    ms = jnp.mean(x * x, axis=-1, keepdims=True)
    return x * lax.rsqrt(ms + EPS) * g


def _sigmoid(x):
    return 0.5 * jnp.tanh(0.5 * x) + 0.5


def _silu(x):
    half = 0.5 * x
    return half * jnp.tanh(half) + half


def _resident(shape):
    nd = len(shape)
    return pl.BlockSpec(shape, lambda *_: (0,) * nd, pipeline_mode=pl.Buffered(1))


def _params(sem):
    return pltpu.CompilerParams(dimension_semantics=sem, vmem_limit_bytes=VMEM_LIMIT)


def _half_step_ffn(x, g_ref, w1_ref, w2_ref, mid_ref):
    xn = _rms(x, g_ref[...]).astype(BF16)
    for c in range(D_FF // FF_CHUNK):
        lo = c * FF_CHUNK
        a = jnp.dot(xn, w1_ref[:, lo:lo + FF_CHUNK], preferred_element_type=F32)
        b = jnp.dot(xn, w1_ref[:, D_FF + lo:D_FF + lo + FF_CHUNK], preferred_element_type=F32)
        mid_ref[:, lo:lo + FF_CHUNK] = (_silu(a) * b).astype(BF16)
    return x + 0.5 * jnp.dot(mid_ref[...], w2_ref[...], preferred_element_type=F32)


def _ffn_kernel(x_ref, g_ref, w1_ref, w2_ref, o_ref, mid_ref):
    o_ref[...] = _half_step_ffn(x_ref[...], g_ref, w1_ref, w2_ref, mid_ref)


def _ffn_specs():
    return [_resident((1, D_MODEL)), _resident((D_MODEL, 2 * D_FF)), _resident((D_FF, D_MODEL))]


def _row(width):
    return pl.BlockSpec((ROW_TILE, width), lambda i: (i, 0))


def _ffn(h, g, w1, w2):
    n = h.shape[0]
    return pl.pallas_call(
        _ffn_kernel,
        grid=(n // ROW_TILE,),
        in_specs=[_row(D_MODEL)] + _ffn_specs(),
        out_specs=_row(D_MODEL),
        out_shape=jax.ShapeDtypeStruct((n, D_MODEL), F32),
        scratch_shapes=[pltpu.VMEM((ROW_TILE, D_FF), BF16)],
        compiler_params=_params(("parallel",)),
        name="ffn",
    )(h, g, w1, w2)


def _mix_ffn_kernel(h_ref, oa_ref, ob_ref, oc_ref, gate_ref, wa_ref, wb_ref, wc_ref, wo_ref,
                    g_ref, w1_ref, w2_ref, *rest, final):
    if final:
        gf_ref, o_ref, mid_ref = rest
    else:
        o_ref, mid_ref = rest
    mixed = None
    for j, (y_ref, w_ref) in enumerate(((oa_ref, wa_ref), (ob_ref, wb_ref), (oc_ref, wc_ref))):
        y = jnp.dot(y_ref[...], w_ref[...], preferred_element_type=F32)
        term = gate_ref[:, j * D_MODEL:(j + 1) * D_MODEL] * y
        mixed = term if mixed is None else mixed + term
    x = h_ref[...] + jnp.dot(mixed.astype(BF16), wo_ref[...], preferred_element_type=F32)
    y = _half_step_ffn(x, g_ref, w1_ref, w2_ref, mid_ref)
    if final:
        y = _rms(y, gf_ref[...])
    o_ref[...] = y


def _mix_ffn(h, oa, ob, oc, gate, wa, wb, wc, wo, g, w1, w2, gf=None):
    n = h.shape[0]
    final = gf is not None
    in_specs = ([_row(D_MODEL), _row(ATT_W), _row(ATT_W), _row(CONV_CH), _row(N_BRANCH * D_MODEL),
                 _resident((ATT_W, D_MODEL)), _resident((ATT_W, D_MODEL)),
                 _resident((CONV_CH, D_MODEL)), _resident((D_MODEL, D_MODEL))] + _ffn_specs())
    args = [h, oa, ob, oc, gate, wa, wb, wc, wo, g, w1, w2]
    if final:
        in_specs.append(_resident((1, D_MODEL)))
        args.append(gf)
    return pl.pallas_call(
        functools.partial(_mix_ffn_kernel, final=final),
        grid=(n // ROW_TILE,),
        in_specs=in_specs,
        out_specs=_row(D_MODEL),
        out_shape=jax.ShapeDtypeStruct((n, D_MODEL), F32),
        scratch_shapes=[pltpu.VMEM((ROW_TILE, D_FF), BF16)],
        compiler_params=_params(("parallel",)),
        name="mix_ffn_final" if final else "mix_ffn",
    )(*args)


def _proj_kernel(h_ref, g_ref, w_ref, gb_ref, wqa_ref, wva_ref, wqb_ref, wvb_ref,
                 qa_ref, ka_ref, va_ref, qb_ref, kb_ref, vb_ref, c_ref, gate_ref):
    u = _rms(h_ref[...], g_ref[...]).astype(BF16)

    def seg(off, width):
        return jnp.dot(u, w_ref[:, off:off + width], preferred_element_type=F32)

    def seg_t(wt_ref, o_ref, scale):
        r = lax.dot_general(wt_ref[...], u, (((1,), (1,)), ((), ())), preferred_element_type=F32)
        r = (r * scale).astype(BF16)
        for j in range(ROW_TILE // ATT_TILE):
            o_ref[0, j] = r[:, j * ATT_TILE:(j + 1) * ATT_TILE]

    for j in range(N_BRANCH):
        lo = j * D_MODEL
        gate_ref[:, lo:lo + D_MODEL] = _sigmoid(
            seg(OFF_GATE + lo, D_MODEL) + gb_ref[:, lo:lo + D_MODEL]).astype(gate_ref.dtype)
    c_ref[...] = seg(OFF_CONV, CONV_CH) * _sigmoid(seg(OFF_CONV + CONV_CH, CONV_CH))
    seg_t(wqa_ref, qa_ref, QK_SCALE)
    seg_t(wva_ref, va_ref, 1.0)
    seg_t(wqb_ref, qb_ref, QK_SCALE)
    seg_t(wvb_ref, vb_ref, 1.0)
    ka_ref[...] = seg(OFF_SBK, ATT_W).astype(BF16)
    kb_ref[...] = seg(OFF_MBK, ATT_W).astype(BF16)


def _proj(h, g, w, gb, wt_qa, wt_va, wt_qb, wt_vb, bsz, seq):
    n = h.shape[0]
    tiles_per_seq = seq // ROW_TILE
    sub = ROW_TILE // ATT_TILE

    def row(width):
        return pl.BlockSpec((ROW_TILE, width), lambda i: (i, 0))

    feat = pl.BlockSpec((1, sub, ATT_W, ATT_TILE),
                        lambda i: (i // tiles_per_seq, i % tiles_per_seq, 0, 0))
    tok_shape = jax.ShapeDtypeStruct((n, ATT_W), BF16)
    feat_shape = jax.ShapeDtypeStruct((bsz, seq // ATT_TILE, ATT_W, ATT_TILE), BF16)
    wt = _resident((ATT_W, D_MODEL))
    return pl.pallas_call(
        _proj_kernel,
        grid=(n // ROW_TILE,),
        in_specs=[row(D_MODEL), _resident((1, D_MODEL)), _resident((D_MODEL, IN_COLS)),
                  _resident((1, N_BRANCH * D_MODEL)), wt, wt, wt, wt],
        out_specs=[feat, row(ATT_W), feat, feat, row(ATT_W), feat, row(CONV_CH),
                   row(N_BRANCH * D_MODEL)],
        out_shape=[feat_shape, tok_shape, feat_shape, feat_shape, tok_shape, feat_shape,
                   jax.ShapeDtypeStruct((n, CONV_CH), F32),
                   jax.ShapeDtypeStruct((n, N_BRANCH * D_MODEL), BF16)],
        compiler_params=_params(("parallel",)),
        name="proj",
    )(h, g, w, gb, wt_qa, wt_va, wt_qb, wt_vb)


def _head_lanes(shape, hh):
    lane = lax.broadcasted_iota(jnp.int32, shape, len(shape) - 1)
    return (lane >= hh * HEAD_DIM) & (lane < (hh + 1) * HEAD_DIM)


def _head_rows(shape, hh):
    row = lax.broadcasted_iota(jnp.int32, shape, 0)
    return (row >= hh * HEAD_DIM) & (row < (hh + 1) * HEAD_DIM)


def _split_bf16(x):
    hi = x.astype(BF16)
    return hi, (x - hi.astype(F32)).astype(BF16)


def _mm(a, b):
    return jnp.dot(a, b, preferred_element_type=F32)


def _att_call(body, name, seq, bsz, extra_in=(), extra_specs=(), scratch=()):
    n_blk = seq // ATT_TILE
    width = PAIRS_PER_STEP * LANES
    feat = pl.BlockSpec((1, n_blk, width, ATT_TILE), lambda b, p: (b, 0, p, 0))
    tok = pl.BlockSpec((1, seq, width), lambda b, p: (b, 0, p))
    return pl.pallas_call(
        functools.partial(body, seq=seq),
        grid=(bsz, N_HEADS // (PAIR * PAIRS_PER_STEP)),
        in_specs=list(extra_specs) + [feat, tok, feat],
        out_specs=tok,
        out_shape=jax.ShapeDtypeStruct((bsz, seq, ATT_W), BF16),
        scratch_shapes=list(scratch),
        compiler_params=_params(("parallel", "parallel")),
        name=name,
    )


def _acc_scratch():
    return pltpu.VMEM((len(_chains()), LANES, ATT_TILE), F32)


def _chains():
    return [(pp, hh, half) for half in range(Q_HALVES) for pp in range(PAIRS_PER_STEP)
            for hh in range(PAIR)]


def _pair_lanes(pp):
    return slice(pp * LANES, (pp + 1) * LANES)


def _store_pair(o_ref, qb, pp, out_t):
    t = ATT_TILE
    o_t = jnp.where(_head_rows((LANES, t), 0), out_t[0], out_t[1])
    q0 = pl.multiple_of(qb * t, t)
    o_ref[0, pl.ds(q0, t), _pair_lanes(pp)] = o_t.T.astype(o_ref.dtype)


def _sb_kernel(qt_ref, k_ref, vt_ref, o_ref, acc_ref, *, seq):
    t = ATT_TILE
    n_sup = seq // (t * Q_HALVES)
    past = (lax.broadcasted_iota(jnp.int32, (t, t), 0)
            < lax.broadcasted_iota(jnp.int32, (t, t), 1))
    r = lax.broadcasted_iota(jnp.int32, (SUFFIX_TILE, SUFFIX_TILE), 0)
    c = lax.broadcasted_iota(jnp.int32, (SUFFIX_TILE, SUFFIX_TILE), 1)
    upper = (r <= c).astype(BF16)
    upper2 = jnp.concatenate([upper, upper], axis=1)
    n_sub = t // SUFFIX_TILE

    def scores(cs, qs, kb):
        k0 = pl.multiple_of(kb * t, t)
        k_t = {pp: k_ref[0, pl.ds(k0, t), _pair_lanes(pp)] for pp in sorted({c[0] for c in cs})}
        return [_mm(k_t[c[0]], q) for c, q in zip(cs, qs)]

    def weights(zs, carries, diags):
        stacks = []
        for z, diag in zip(zs, diags):
            sp = jnp.maximum(z, 0.0) + jnp.log(1.0 + jnp.exp(-jnp.abs(z)))
            if diag:
                sp = jnp.where(past, sp, 0.0)
            hi, lo = _split_bf16(sp)
            stacks.append([jnp.concatenate([hi[s * SUFFIX_TILE:(s + 1) * SUFFIX_TILE],
                                            lo[s * SUFFIX_TILE:(s + 1) * SUFFIX_TILE]], axis=0)
                           for s in range(n_sub)])
        incls = [[_mm(upper2, st) for st in stack] for stack in stacks]
        ws, new_carries = [], []
        for z, incl, carry, diag in zip(zs, incls, carries, diags):
            parts = [None] * n_sub
            for s in reversed(range(n_sub)):
                parts[s] = jnp.exp(z[s * SUFFIX_TILE:(s + 1) * SUFFIX_TILE] - incl[s] - carry)
                carry = carry + incl[s][0:1, :]
            w = jnp.concatenate(parts, axis=0)
            ws.append((jnp.where(past, w, 0.0) if diag else w).astype(BF16))
            new_carries.append(carry)
        return ws, new_carries

    chains = _chains()

    def values(ws, idx, kb, first=None):
        for i, (n, w) in enumerate(zip(idx, ws)):
            pv = _mm(vt_ref[0, kb, _pair_lanes(chains[n][0]), :], w)
            acc_ref[n] = pv if first is not None and first[i] else acc_ref[n] + pv

    def sup_block(sb, _):
        qbs = [sb * Q_HALVES + i for i in range(Q_HALVES)]
        q = {}
        for pp, hh, half in chains:
            q_pair = qt_ref[0, qbs[half], _pair_lanes(pp), :]
            q[pp, hh, half] = jnp.where(_head_rows((LANES, t), hh), q_pair, jnp.zeros_like(q_pair))
        carry = {c: jnp.zeros((1, t), F32) for c in chains}
        for kk in reversed(range(Q_HALVES)):
            active = [c for c in chains if c[2] >= kk]
            diags = [c[2] == kk for c in active]
            ws, cs = weights(scores(active, [q[c] for c in active], qbs[kk]),
                             [carry[c] for c in active], diags)
            carry.update(zip(active, cs))
            values(ws, [chains.index(c) for c in active], qbs[kk], first=diags)

        def unsaturated(carries):
            return (jnp.min(functools.reduce(jnp.minimum, carries)) < SATURATED).astype(jnp.int32)

        qs = [q[c] for c in chains]

        def body(st):
            i, carries = st[0], st[2:]
            kb = qbs[0] - 1 - i
            ws, carries = weights(scores(chains, qs, kb), carries, [False] * len(chains))
            values(ws, range(len(chains)), kb)
            return (i + 1, unsaturated(carries), *carries)

        carries = [carry[c] for c in chains]
        lax.while_loop(lambda st: (st[0] < qbs[0]) & (st[1] > 0), body,
                       (jnp.int32(0), unsaturated(carries), *carries))
        for half in range(Q_HALVES):
            for pp in range(PAIRS_PER_STEP):
                _store_pair(o_ref, qbs[half], pp,
                            [acc_ref[chains.index((pp, hh, half))] for hh in range(PAIR)])
        return 0

    lax.fori_loop(0, n_sup, sup_block, 0)


def _sb_attention(qt, k, vt):
    bsz, seq, _ = k.shape
    return _att_call(_sb_kernel, "stickbreak", seq, bsz, scratch=[_acc_scratch()])(qt, k, vt)


def _moba_kernel(slope_ref, qt_ref, k_ref, vt_ref, o_ref, acc_ref, spre_ref, plate_ref, vaug_ref,
                 *, seq):
    t = ATT_TILE
    n_blk = seq // t
    n_sup = n_blk // Q_HALVES
    km_rows = max(n_blk, SUBLANES_BF16)
    causal = (lax.broadcasted_iota(jnp.int32, (t, t), 0)
              <= lax.broadcasted_iota(jnp.int32, (t, t), 1))
    klane = lax.broadcasted_iota(jnp.int32, (t, LANES), 1)
    krow = lax.broadcasted_iota(jnp.int32, (t, LANES), 0).astype(F32)
    qrow = lax.broadcasted_iota(jnp.int32, (LANES, t), 0)
    qcol = lax.broadcasted_iota(jnp.int32, (LANES, t), 1).astype(F32)
    cand = lax.broadcasted_iota(jnp.int32, (n_blk, t), 0)
    extra0 = [(1 - hh) * HEAD_DIM for hh in range(PAIR)]
    slopes, km_split, k_fill = {}, {}, {}
    for pp in range(PAIRS_PER_STEP):
        kmean = jnp.concatenate(
            [jnp.mean(k_ref[0, j * t:(j + 1) * t, _pair_lanes(pp)].astype(F32), axis=0,
                      keepdims=True) for j in range(n_blk)]
            + ([jnp.zeros((km_rows - n_blk, LANES), F32)] if km_rows > n_blk else []), axis=0)
        for hh in range(PAIR):
            slope = slope_ref[(pl.program_id(1) * PAIRS_PER_STEP + pp) * PAIR + hh]
            slopes[pp, hh] = slope
            km_split[pp, hh] = _split_bf16(
                jnp.where(_head_lanes((km_rows, LANES), hh), kmean, 0.0))
            x0 = extra0[hh]
            k_fill[pp, hh] = jnp.where(
                klane == x0 + n_blk, 1.0,
                jnp.where(klane == x0 + n_blk + 1, slope * krow, 0.0)).astype(BF16)
            for j in range(n_blk):
                vaug_ref[pp * PAIR + hh, j] = jnp.where(
                    qrow == x0, jnp.ones((), BF16), vt_ref[0, j, _pair_lanes(pp), :])

    def q_with_extras(pp, hh, qb):
        q_pair = qt_ref[0, qb, _pair_lanes(pp), :]
        hi, lo = km_split[pp, hh]
        gate = (_mm(hi, q_pair) + _mm(lo, q_pair))[:n_blk]
        valid = cand < qb
        gate = jnp.where(valid, gate, -jnp.inf)
        rank = jnp.zeros((n_blk, t), jnp.int32)
        for j in range(n_blk):
            other = gate[j:j + 1, :]
            ahead = (other > gate) | ((other == gate) & (j < cand))
            rank = rank + ahead.astype(jnp.int32)
        unsel = jnp.where(valid & (rank < MOBA_TOPK), 0.0, 1.0)
        x0 = extra0[hh]
        pieces = [unsel, jnp.zeros((LANES - x0 - n_blk, t), F32)]
        if x0:
            pieces.insert(0, jnp.zeros((x0, t), F32))
        extras = jnp.concatenate(pieces, axis=0)
        extras = jnp.where(qrow == x0 + n_blk, -slopes[pp, hh] * qcol,
                           jnp.where(qrow == x0 + n_blk + 1, 1.0, extras))
        return jnp.where(_head_rows((LANES, t), hh), q_pair, extras.astype(BF16))

    def k_with_extras(pp, hh, kb, diag):
        k0 = pl.multiple_of(kb * t, t)
        k_pair = k_ref[0, pl.ds(k0, t), _pair_lanes(pp)]
        fill = k_fill[pp, hh]
        if not diag:
            fill = jnp.where(klane == extra0[hh] + kb, jnp.full((), -MASK_BIG, BF16), fill)
        return jnp.where(_head_lanes((t, LANES), hh), k_pair, fill)

    def scores(chains, qs, kb, diags):
        k_x = {phd: k_with_extras(*phd[:2], kb, phd[2])
               for phd in sorted({(c[0], c[1], d) for c, d in zip(chains, diags)})}
        return [_mm(k_x[c[0], c[1], d], q) for c, q, d in zip(chains, qs, diags)]

    def probs(chains, ss, qbs, kb, ms, diags):
        ps, alphas, new_ms = [], [], []
        for (pp, hh, half), s, m, diag in zip(chains, ss, ms, diags):
            if diag:
                s = jnp.where(causal, s, NEG)
                shift = 0.0
            else:
                shift = slopes[pp, hh] * jnp.asarray((qbs[half] - kb) * t, F32)
            m_new = jnp.maximum(m, jnp.max(s, axis=0, keepdims=True) - shift)
            alphas.append(jnp.exp(m - m_new))
            ps.append(jnp.exp(s - (m_new + shift)).astype(BF16))
            new_ms.append(m_new)
        return ps, alphas, new_ms

    chains = _chains()

    def values(ps, alphas, idx, kb, first=None):
        for i, (n, p, alpha) in enumerate(zip(idx, ps, alphas)):
            pv = _mm(vaug_ref[chains[n][0] * PAIR + chains[n][1], kb], p)
            acc_ref[n] = pv if first is not None and first[i] else alpha * acc_ref[n] + pv

    def sup_block(sb, _):
        qbs = [sb * Q_HALVES + i for i in range(Q_HALVES)]
        q = {c: q_with_extras(c[0], c[1], qbs[c[2]]) for c in chains}
        m = {c: jnp.full((1, t), NEG, F32) for c in chains}
        for kk in reversed(range(Q_HALVES)):
            active = [c for c in chains if c[2] >= kk]
            diags = [c[2] == kk for c in active]
            ss = scores(active, [q[c] for c in active], qbs[kk], diags)
            ps, alphas, ms = probs(active, ss, qbs, qbs[kk], [m[c] for c in active], diags)
            m.update(zip(active, ms))
            values(ps, alphas, [chains.index(c) for c in active], qbs[kk], first=diags)

        for pp in range(PAIRS_PER_STEP):
            sub = [c for c in chains if c[0] == pp]
            idx = [chains.index(c) for c in sub]
            qs = [q[c] for c in sub]
            n_sub = len(sub)
            n_now = n_sub - N_DEFER

            def prefetch(kb, sub=sub, qs=qs):
                ss = scores(sub[:N_PREFETCH], qs[:N_PREFETCH], jnp.maximum(kb, 0),
                            [False] * N_PREFETCH)
                for j, s in enumerate(ss):
                    spre_ref[j] = s

            prefetch(qbs[0] - 1)
            for j in range(N_DEFER):
                plate_ref[j] = jnp.zeros((t, t), BF16)

            def body(i, st, sub=sub, idx=idx, qs=qs, n_sub=n_sub, n_now=n_now, prefetch=prefetch):
                kb = qbs[0] - 1 - i
                ms, alphas_late = list(st[:n_sub]), list(st[n_sub:])
                ss = ([spre_ref[j] for j in range(N_PREFETCH)]
                      + scores(sub[N_PREFETCH:], qs[N_PREFETCH:], kb, [False] * (n_sub - N_PREFETCH)))
                prefetch(kb - 1)
                values([plate_ref[j] for j in range(N_DEFER)], alphas_late, idx[n_now:], kb + 1)
                ps, alphas, ms = probs(sub, ss, qbs, kb, ms, [False] * n_sub)
                values(ps[:n_now], alphas[:n_now], idx[:n_now], kb)
                for j in range(N_DEFER):
                    plate_ref[j] = ps[n_now + j]
                return (*ms, *alphas[n_now:])

            st = lax.fori_loop(0, qbs[0], body,
                               (*[m[c] for c in sub], *[jnp.ones((1, t), F32)] * N_DEFER))
            values([plate_ref[j] for j in range(N_DEFER)], st[n_sub:], idx[n_now:], 0)
        for half in range(Q_HALVES):
            for pp in range(PAIRS_PER_STEP):
                outs = []
                for hh in range(PAIR):
                    acc = acc_ref[chains.index((pp, hh, half))]
                    outs.append(acc / acc[extra0[hh]:extra0[hh] + 1, :])
                _store_pair(o_ref, qbs[half], pp, outs)
        return 0

    lax.fori_loop(0, n_sup, sup_block, 0)


def _moba_attention(qt, k, vt, slopes):
    bsz, seq, _ = k.shape
    n_blk = seq // MOBA_BLOCK
    assert ATT_TILE == MOBA_BLOCK and HEAD_DIM % n_blk == 0 and n_blk + 2 <= HEAD_DIM
    call = _att_call(_moba_kernel, "moba", seq, bsz,
                     extra_specs=[pl.BlockSpec(memory_space=pltpu.SMEM)],
                     scratch=[_acc_scratch(),
                              pltpu.VMEM((N_PREFETCH, ATT_TILE, ATT_TILE), F32),
                              pltpu.VMEM((N_DEFER, ATT_TILE, ATT_TILE), BF16),
                              pltpu.VMEM((PAIRS_PER_STEP * PAIR, n_blk, LANES, ATT_TILE), BF16)])
    return call(slopes, qt, k, vt)


def _conv_kernel(c_ref, w_ref, b_ref, g_ref, beta_ref, o_ref, pad_ref, *, seq):
    pad_ref[0:CONV_HALO, :] = jnp.zeros((CONV_HALO, CONV_CH), F32)
    pad_ref[CONV_HALO:, :] = c_ref[0]
    rows = CONV_TILE + CONV_HALO
    lead = CONV_HALO - (CONV_WIDTH - 1)

    def step(i, _):
        r0 = pl.multiple_of(i * CONV_TILE, CONV_TILE)
        chunks = []
        for ch in range(CONV_CH // LANES):
            lanes = slice(ch * LANES, (ch + 1) * LANES)
            x = pad_ref[pl.ds(r0, rows), lanes]
            acc = jnp.zeros((CONV_TILE, LANES), F32) + b_ref[:, lanes]
            for sub in range(8):
                xs = x if sub == 0 else pltpu.roll(x, rows - sub, axis=0)
                for tap in range(CONV_WIDTH):
                    off = tap + lead
                    if off % 8 == sub:
                        base = off - sub
                        acc = acc + xs[base:base + CONV_TILE, :] * w_ref[tap:tap + 1, lanes]
            chunks.append(acc)
        acc = jnp.concatenate(chunks, axis=1)
        mu = jnp.mean(acc, axis=-1, keepdims=True)
        d = acc - mu
        var = jnp.mean(d * d, axis=-1, keepdims=True)
        y = d * lax.rsqrt(var + EPS) * g_ref[...] + beta_ref[...]
        o_ref[0, pl.ds(r0, CONV_TILE), :] = _silu(y).astype(o_ref.dtype)
        return 0

    lax.fori_loop(0, seq // CONV_TILE, step, 0)


def _conv(c, w, b, g, beta):
    bsz, seq, _ = c.shape
    blk = pl.BlockSpec((1, seq, CONV_CH), lambda i: (i, 0, 0))
    return pl.pallas_call(
        functools.partial(_conv_kernel, seq=seq),
        grid=(bsz,),
        in_specs=[blk, _resident((CONV_WIDTH, CONV_CH)), _resident((1, CONV_CH)),
                  _resident((1, CONV_CH)), _resident((1, CONV_CH))],
        out_specs=blk,
        out_shape=jax.ShapeDtypeStruct((bsz, seq, CONV_CH), BF16),
        scratch_shapes=[pltpu.VMEM((seq + CONV_HALO, CONV_CH), F32)],
        compiler_params=_params(("parallel",)),
        name="conv",
    )(c, w, b, g, beta)


def kernel(x, ffn1_norm, ffn1_w_in, ffn1_w_out, mix_norm, w_in, gate_bias, sb_w_out, mb_w_out,
           conv_dw, conv_dw_bias, conv_ln_g, conv_ln_b, conv_w_out, w_o, ffn2_norm, ffn2_w_in,
           ffn2_w_out, final_norm):
    bsz, seq, _ = x.shape
    n = bsz * seq
    depth = w_in.shape[0]
    assert seq % ROW_TILE == 0 and seq % (ATT_TILE * Q_HALVES) == 0 and seq % CONV_TILE == 0
    slopes_np = np.exp2(-8.0 * np.arange(1, N_HEADS + 1, dtype=np.float32) / N_HEADS)
    assert all(math.frexp(float(s))[0] == 0.5 for s in slopes_np) and ATT_TILE <= 256
    slopes = jnp.asarray(slopes_np)

    def vec(a):
        return a.reshape(1, -1)

    def seq3(a):
        return a.reshape(bsz, seq, a.shape[-1])

    h = x.reshape(n, D_MODEL)
    for l in range(depth):
        h = _ffn(h, vec(ffn1_norm[l]), ffn1_w_in[l].astype(BF16), ffn1_w_out[l].astype(BF16))
        w_l = w_in[l].astype(BF16)
        wt = [w_l[:, off:off + ATT_W].T for off in (OFF_SBQ, OFF_SBV, OFF_MBQ, OFF_MBV)]
        qa_t, ka, va_t, qb_t, kb, vb_t, c, gate = _proj(h, vec(mix_norm[l]), w_l,
                                                        vec(gate_bias[l]), *wt, bsz, seq)
        oa = _sb_attention(qa_t, seq3(ka), va_t)
        ob = _moba_attention(qb_t, seq3(kb), vb_t, slopes)
        oc = _conv(seq3(c), conv_dw[l].reshape(CONV_WIDTH, CONV_CH), vec(conv_dw_bias[l]),
                   vec(conv_ln_g[l]), vec(conv_ln_b[l]))
        last = l == depth - 1
        h = _mix_ffn(h, oa.reshape(n, ATT_W), ob.reshape(n, ATT_W), oc.reshape(n, CONV_CH), gate,
                     sb_w_out[l].astype(BF16), mb_w_out[l].astype(BF16),
                     conv_w_out[l].astype(BF16), w_o[l].astype(BF16), vec(ffn2_norm[l]),
                     ffn2_w_in[l].astype(BF16), ffn2_w_out[l].astype(BF16),
                     vec(final_norm) if last else None)
    return h.reshape(bsz, seq, D_MODEL)
```

```python
import functools
import math

import jax
import jax.numpy as jnp
import numpy as np
from jax import lax
from jax.experimental import pallas as pl
from jax.experimental.pallas import tpu as pltpu

D_MODEL = 1024
HEAD_DIM = 64
N_HEADS = 8
ATT_W = N_HEADS * HEAD_DIM
CONV_CH = D_MODEL // 2
CONV_WIDTH = 31
MOBA_BLOCK = 256
MOBA_TOPK = 3
D_FF = 2816
N_BRANCH = 3
EPS = 1e-6
QK_SCALE = HEAD_DIM ** -0.5

OFF_SBQ = 0
OFF_SBK = OFF_SBQ + ATT_W
OFF_SBV = OFF_SBK + ATT_W
OFF_MBQ = OFF_SBV + ATT_W
OFF_MBK = OFF_MBQ + ATT_W
OFF_MBV = OFF_MBK + ATT_W
OFF_CONV = OFF_MBV + ATT_W
OFF_GATE = OFF_CONV + 2 * CONV_CH
IN_COLS = OFF_GATE + N_BRANCH * D_MODEL

LANES = 128
SUBLANES_BF16 = 16
PAIR = LANES // HEAD_DIM
FF_CHUNK = 256
ROW_TILE = 512
ATT_TILE = 256
Q_HALVES = 2
PAIRS_PER_STEP = 2
SUFFIX_TILE = 128
N_PREFETCH = 2
N_DEFER = 2
CONV_TILE = 128
CONV_HALO = 32
VMEM_LIMIT = 56 * 1024 * 1024
SATURATED = 128.0
MASK_BIG = 2.0 ** 100
NEG = -1e30

F32 = jnp.float32
BF16 = jnp.bfloat16


def _rms(x, g):
    ms = jnp.mean(x * x, axis=-1, keepdims=True)
    return x * lax.rsqrt(ms + EPS) * g


def _sigmoid(x):
    return 0.5 * jnp.tanh(0.5 * x) + 0.5


def _silu(x):
    half = 0.5 * x
    return half * jnp.tanh(half) + half


def _resident(shape):
    nd = len(shape)
    return pl.BlockSpec(shape, lambda *_: (0,) * nd, pipeline_mode=pl.Buffered(1))


def _params(sem):
    return pltpu.CompilerParams(dimension_semantics=sem, vmem_limit_bytes=VMEM_LIMIT)


def _half_step_ffn(x, g_ref, w1_ref, w2_ref, mid_ref):
    xn = _rms(x, g_ref[...]).astype(BF16)
    for c in range(D_FF // FF_CHUNK):
        lo = c * FF_CHUNK
        a = jnp.dot(xn, w1_ref[:, lo:lo + FF_CHUNK], preferred_element_type=F32)
        b = jnp.dot(xn, w1_ref[:, D_FF + lo:D_FF + lo + FF_CHUNK], preferred_element_type=F32)
        mid_ref[:, lo:lo + FF_CHUNK] = (_silu(a) * b).astype(BF16)
    return x + 0.5 * jnp.dot(mid_ref[...], w2_ref[...], preferred_element_type=F32)


def _ffn_kernel(x_ref, g_ref, w1_ref, w2_ref, o_ref, mid_ref):
    o_ref[...] = _half_step_ffn(x_ref[...], g_ref, w1_ref, w2_ref, mid_ref)


def _ffn_specs():
    return [_resident((1, D_MODEL)), _resident((D_MODEL, 2 * D_FF)), _resident((D_FF, D_MODEL))]


def _row(width):
    return pl.BlockSpec((ROW_TILE, width), lambda i: (i, 0))


def _ffn(h, g, w1, w2):
    n = h.shape[0]
    return pl.pallas_call(
        _ffn_kernel,
        grid=(n // ROW_TILE,),
        in_specs=[_row(D_MODEL)] + _ffn_specs(),
        out_specs=_row(D_MODEL),
        out_shape=jax.ShapeDtypeStruct((n, D_MODEL), F32),
        scratch_shapes=[pltpu.VMEM((ROW_TILE, D_FF), BF16)],
        compiler_params=_params(("parallel",)),
        name="ffn",
    )(h, g, w1, w2)


def _mix_ffn_kernel(h_ref, oa_ref, ob_ref, oc_ref, gate_ref, wa_ref, wb_ref, wc_ref, wo_ref,
                    g_ref, w1_ref, w2_ref, *rest, final):
    if final:
        gf_ref, o_ref, mid_ref = rest
    else:
        o_ref, mid_ref = rest
    mixed = None
    for j, (y_ref, w_ref) in enumerate(((oa_ref, wa_ref), (ob_ref, wb_ref), (oc_ref, wc_ref))):
        y = jnp.dot(y_ref[...], w_ref[...], preferred_element_type=F32)
        term = gate_ref[:, j * D_MODEL:(j + 1) * D_MODEL] * y
        mixed = term if mixed is None else mixed + term
    x = h_ref[...] + jnp.dot(mixed.astype(BF16), wo_ref[...], preferred_element_type=F32)
    y = _half_step_ffn(x, g_ref, w1_ref, w2_ref, mid_ref)
    if final:
        y = _rms(y, gf_ref[...])
    o_ref[...] = y


def _mix_ffn(h, oa, ob, oc, gate, wa, wb, wc, wo, g, w1, w2, gf=None):
    n = h.shape[0]
    final = gf is not None
    in_specs = ([_row(D_MODEL), _row(ATT_W), _row(ATT_W), _row(CONV_CH), _row(N_BRANCH * D_MODEL),
                 _resident((ATT_W, D_MODEL)), _resident((ATT_W, D_MODEL)),
                 _resident((CONV_CH, D_MODEL)), _resident((D_MODEL, D_MODEL))] + _ffn_specs())
    args = [h, oa, ob, oc, gate, wa, wb, wc, wo, g, w1, w2]
    if final:
        in_specs.append(_resident((1, D_MODEL)))
        args.append(gf)
    return pl.pallas_call(
        functools.partial(_mix_ffn_kernel, final=final),
        grid=(n // ROW_TILE,),
        in_specs=in_specs,
        out_specs=_row(D_MODEL),
        out_shape=jax.ShapeDtypeStruct((n, D_MODEL), F32),
        scratch_shapes=[pltpu.VMEM((ROW_TILE, D_FF), BF16)],
        compiler_params=_params(("parallel",)),
        name="mix_ffn_final" if final else "mix_ffn",
    )(*args)


def _conv_rows(pad_ref, r0, w_ref, b_ref, g_ref, beta_ref):
    lead = CONV_HALO - (CONV_WIDTH - 1)
    rows = CONV_TILE + CONV_HALO
    chunks = []
    for ch in range(CONV_CH // LANES):
        lanes = slice(ch * LANES, (ch + 1) * LANES)
        x = pad_ref[r0:r0 + rows, lanes]
        acc = jnp.zeros((CONV_TILE, LANES), F32) + b_ref[:, lanes]
        for sub in range(8):
            xs = x if sub == 0 else pltpu.roll(x, rows - sub, axis=0)
            for tap in range(CONV_WIDTH):
                off = tap + lead
                if off % 8 == sub:
                    base = off - sub
                    acc = acc + xs[base:base + CONV_TILE, :] * w_ref[tap:tap + 1, lanes]
        chunks.append(acc)
    acc = jnp.concatenate(chunks, axis=1)
    mu = jnp.mean(acc, axis=-1, keepdims=True)
    d = acc - mu
    var = jnp.mean(d * d, axis=-1, keepdims=True)
    return _silu(d * lax.rsqrt(var + EPS) * g_ref[...] + beta_ref[...]).astype(BF16)


def _proj_kernel(h_ref, g_ref, w_ref, gb_ref, wqa_ref, wva_ref, wqb_ref, wvb_ref,
                 cw_ref, cb_ref, cg_ref, cbeta_ref,
                 qa_ref, ka_ref, va_ref, qb_ref, kb_ref, vb_ref, oc_ref, gate_ref, pad_ref,
                 *, tiles_per_seq):
    u = _rms(h_ref[...], g_ref[...]).astype(BF16)

    def seg(off, width):
        return jnp.dot(u, w_ref[:, off:off + width], preferred_element_type=F32)

    def seg_t(wt_ref, o_ref, scale):
        r = lax.dot_general(wt_ref[...], u, (((1,), (1,)), ((), ())), preferred_element_type=F32)
        r = (r * scale).astype(BF16)
        for j in range(ROW_TILE // ATT_TILE):
            o_ref[0, j] = r[:, j * ATT_TILE:(j + 1) * ATT_TILE]

    @pl.when(pl.program_id(0) % tiles_per_seq == 0)
    def _():
        pad_ref[0:CONV_HALO, :] = jnp.zeros((CONV_HALO, CONV_CH), F32)

    pad_ref[CONV_HALO:, :] = seg(OFF_CONV, CONV_CH) * _sigmoid(seg(OFF_CONV + CONV_CH, CONV_CH))

    for r0 in range(0, ROW_TILE, CONV_TILE):
        oc_ref[r0:r0 + CONV_TILE, :] = _conv_rows(pad_ref, r0, cw_ref, cb_ref, cg_ref, cbeta_ref)
    pad_ref[0:CONV_HALO, :] = pad_ref[ROW_TILE:ROW_TILE + CONV_HALO, :]
    for j in range(N_BRANCH):
        lo = j * D_MODEL
        gate_ref[:, lo:lo + D_MODEL] = _sigmoid(
            seg(OFF_GATE + lo, D_MODEL) + gb_ref[:, lo:lo + D_MODEL]).astype(gate_ref.dtype)
    seg_t(wqa_ref, qa_ref, QK_SCALE)
    seg_t(wva_ref, va_ref, 1.0)
    seg_t(wqb_ref, qb_ref, QK_SCALE)
    seg_t(wvb_ref, vb_ref, 1.0)
    ka_ref[...] = seg(OFF_SBK, ATT_W).astype(BF16)
    kb_ref[...] = seg(OFF_MBK, ATT_W).astype(BF16)


def _proj(h, g, w, gb, wt_qa, wt_va, wt_qb, wt_vb, conv_w, conv_b, conv_g, conv_beta, bsz, seq):
    n = h.shape[0]
    tiles_per_seq = seq // ROW_TILE
    sub = ROW_TILE // ATT_TILE
    feat = pl.BlockSpec((1, sub, ATT_W, ATT_TILE),
                        lambda i: (i // tiles_per_seq, i % tiles_per_seq, 0, 0))
    tok_shape = jax.ShapeDtypeStruct((n, ATT_W), BF16)
    feat_shape = jax.ShapeDtypeStruct((bsz, seq // ATT_TILE, ATT_W, ATT_TILE), BF16)
    wt = _resident((ATT_W, D_MODEL))
    vec = _resident((1, CONV_CH))
    return pl.pallas_call(
        functools.partial(_proj_kernel, tiles_per_seq=tiles_per_seq),
        grid=(n // ROW_TILE,),
        in_specs=[_row(D_MODEL), _resident((1, D_MODEL)), _resident((D_MODEL, IN_COLS)),
                  _resident((1, N_BRANCH * D_MODEL)), wt, wt, wt, wt,
                  _resident((CONV_WIDTH, CONV_CH)), vec, vec, vec],
        out_specs=[feat, _row(ATT_W), feat, feat, _row(ATT_W), feat, _row(CONV_CH),
                   _row(N_BRANCH * D_MODEL)],
        out_shape=[feat_shape, tok_shape, feat_shape, feat_shape, tok_shape, feat_shape,
                   jax.ShapeDtypeStruct((n, CONV_CH), BF16),
                   jax.ShapeDtypeStruct((n, N_BRANCH * D_MODEL), BF16)],
        scratch_shapes=[pltpu.VMEM((CONV_HALO + ROW_TILE, CONV_CH), F32)],
        compiler_params=_params(("arbitrary",)),
        name="proj",
    )(h, g, w, gb, wt_qa, wt_va, wt_qb, wt_vb, conv_w, conv_b, conv_g, conv_beta)


def _head_lanes(shape, hh):
    lane = lax.broadcasted_iota(jnp.int32, shape, len(shape) - 1)
    return (lane >= hh * HEAD_DIM) & (lane < (hh + 1) * HEAD_DIM)


def _head_rows(shape, hh):
    row = lax.broadcasted_iota(jnp.int32, shape, 0)
    return (row >= hh * HEAD_DIM) & (row < (hh + 1) * HEAD_DIM)


def _split_bf16(x):
    hi = x.astype(BF16)
    return hi, (x - hi.astype(F32)).astype(BF16)


def _mm(a, b):
    return jnp.dot(a, b, preferred_element_type=F32)


def _att_call(body, name, seq, bsz, extra_in=(), extra_specs=(), scratch=()):
    n_blk = seq // ATT_TILE
    width = PAIRS_PER_STEP * LANES
    feat = pl.BlockSpec((1, n_blk, width, ATT_TILE), lambda b, p: (b, 0, p, 0))
    tok = pl.BlockSpec((1, seq, width), lambda b, p: (b, 0, p))
    return pl.pallas_call(
        functools.partial(body, seq=seq),
        grid=(bsz, N_HEADS // (PAIR * PAIRS_PER_STEP)),
        in_specs=list(extra_specs) + [feat, tok, feat],
        out_specs=tok,
        out_shape=jax.ShapeDtypeStruct((bsz, seq, ATT_W), BF16),
        scratch_shapes=list(scratch),
        compiler_params=_params(("parallel", "parallel")),
        name=name,
    )


def _acc_scratch():
    return pltpu.VMEM((len(_chains()), LANES, ATT_TILE), F32)


def _chains():
    return [(pp, hh, half) for half in range(Q_HALVES) for pp in range(PAIRS_PER_STEP)
            for hh in range(PAIR)]


def _pair_lanes(pp):
    return slice(pp * LANES, (pp + 1) * LANES)


def _store_pair(o_ref, qb, pp, out_t):
    t = ATT_TILE
    o_t = jnp.where(_head_rows((LANES, t), 0), out_t[0], out_t[1])
    q0 = pl.multiple_of(qb * t, t)
    o_ref[0, pl.ds(q0, t), _pair_lanes(pp)] = o_t.T.astype(o_ref.dtype)


def _sb_kernel(qt_ref, k_ref, vt_ref, o_ref, acc_ref, *, seq):
    t = ATT_TILE
    n_sup = seq // (t * Q_HALVES)
    past = (lax.broadcasted_iota(jnp.int32, (t, t), 0)
            < lax.broadcasted_iota(jnp.int32, (t, t), 1))
    r = lax.broadcasted_iota(jnp.int32, (SUFFIX_TILE, SUFFIX_TILE), 0)
    c = lax.broadcasted_iota(jnp.int32, (SUFFIX_TILE, SUFFIX_TILE), 1)
    upper = (r <= c).astype(BF16)
    upper2 = jnp.concatenate([upper, upper], axis=1)
    n_sub = t // SUFFIX_TILE

    def scores(cs, qs, kb):
        k0 = pl.multiple_of(kb * t, t)
        k_t = {pp: k_ref[0, pl.ds(k0, t), _pair_lanes(pp)] for pp in sorted({c[0] for c in cs})}
        return [_mm(k_t[c[0]], q) for c, q in zip(cs, qs)]

    def weights(zs, carries, diags):
        stacks = []
        for z, diag in zip(zs, diags):
            sp = jnp.maximum(z, 0.0) + jnp.log(1.0 + jnp.exp(-jnp.abs(z)))
            if diag:
                sp = jnp.where(past, sp, 0.0)
            hi, lo = _split_bf16(sp)
            stacks.append([jnp.concatenate([hi[s * SUFFIX_TILE:(s + 1) * SUFFIX_TILE],
                                            lo[s * SUFFIX_TILE:(s + 1) * SUFFIX_TILE]], axis=0)
                           for s in range(n_sub)])
        incls = [[_mm(upper2, st) for st in stack] for stack in stacks]
        ws, new_carries = [], []
        for z, incl, carry, diag in zip(zs, incls, carries, diags):
            parts = [None] * n_sub
            for s in reversed(range(n_sub)):
                parts[s] = jnp.exp(z[s * SUFFIX_TILE:(s + 1) * SUFFIX_TILE] - incl[s] - carry)
                carry = carry + incl[s][0:1, :]
            w = jnp.concatenate(parts, axis=0)
            ws.append((jnp.where(past, w, 0.0) if diag else w).astype(BF16))
            new_carries.append(carry)
        return ws, new_carries

    chains = _chains()

    def values(ws, idx, kb, first=None):
        for i, (n, w) in enumerate(zip(idx, ws)):
            pv = _mm(vt_ref[0, kb, _pair_lanes(chains[n][0]), :], w)
            acc_ref[n] = pv if first is not None and first[i] else acc_ref[n] + pv

    def sup_block(sb, _):
        qbs = [sb * Q_HALVES + i for i in range(Q_HALVES)]
        q = {}
        for pp, hh, half in chains:
            q_pair = qt_ref[0, qbs[half], _pair_lanes(pp), :]
            q[pp, hh, half] = jnp.where(_head_rows((LANES, t), hh), q_pair, jnp.zeros_like(q_pair))
        carry = {c: jnp.zeros((1, t), F32) for c in chains}
        for kk in reversed(range(Q_HALVES)):
            active = [c for c in chains if c[2] >= kk]
            diags = [c[2] == kk for c in active]
            ws, cs = weights(scores(active, [q[c] for c in active], qbs[kk]),
                             [carry[c] for c in active], diags)
            carry.update(zip(active, cs))
            values(ws, [chains.index(c) for c in active], qbs[kk], first=diags)

        def unsaturated(carries):
            return (jnp.min(functools.reduce(jnp.minimum, carries)) < SATURATED).astype(jnp.int32)

        qs = [q[c] for c in chains]

        def body(st):
            i, carries = st[0], st[2:]
            kb = qbs[0] - 1 - i
            ws, carries = weights(scores(chains, qs, kb), carries, [False] * len(chains))
            values(ws, range(len(chains)), kb)
            return (i + 1, unsaturated(carries), *carries)

        carries = [carry[c] for c in chains]
        lax.while_loop(lambda st: (st[0] < qbs[0]) & (st[1] > 0), body,
                       (jnp.int32(0), unsaturated(carries), *carries))
        for half in range(Q_HALVES):
            for pp in range(PAIRS_PER_STEP):
                _store_pair(o_ref, qbs[half], pp,
                            [acc_ref[chains.index((pp, hh, half))] for hh in range(PAIR)])
        return 0

    lax.fori_loop(0, n_sup, sup_block, 0)


def _sb_attention(qt, k, vt):
    bsz, seq, _ = k.shape
    return _att_call(_sb_kernel, "stickbreak", seq, bsz, scratch=[_acc_scratch()])(qt, k, vt)


def _moba_kernel(slope_ref, qt_ref, k_ref, vt_ref, o_ref, acc_ref, spre_ref, plate_ref, vaug_ref,
                 *, seq):
    t = ATT_TILE
    n_blk = seq // t
    n_sup = n_blk // Q_HALVES
    km_rows = max(n_blk, SUBLANES_BF16)
    causal = (lax.broadcasted_iota(jnp.int32, (t, t), 0)
              <= lax.broadcasted_iota(jnp.int32, (t, t), 1))
    klane = lax.broadcasted_iota(jnp.int32, (t, LANES), 1)
    krow = lax.broadcasted_iota(jnp.int32, (t, LANES), 0).astype(F32)
    qrow = lax.broadcasted_iota(jnp.int32, (LANES, t), 0)
    qcol = lax.broadcasted_iota(jnp.int32, (LANES, t), 1).astype(F32)
    cand = lax.broadcasted_iota(jnp.int32, (n_blk, t), 0)
    extra0 = [(1 - hh) * HEAD_DIM for hh in range(PAIR)]
    slopes, km_split, k_fill = {}, {}, {}
    for pp in range(PAIRS_PER_STEP):
        kmean = jnp.concatenate(
            [jnp.mean(k_ref[0, j * t:(j + 1) * t, _pair_lanes(pp)].astype(F32), axis=0,
                      keepdims=True) for j in range(n_blk)]
            + ([jnp.zeros((km_rows - n_blk, LANES), F32)] if km_rows > n_blk else []), axis=0)
        for hh in range(PAIR):
            slope = slope_ref[(pl.program_id(1) * PAIRS_PER_STEP + pp) * PAIR + hh]
            slopes[pp, hh] = slope
            km_split[pp, hh] = _split_bf16(
                jnp.where(_head_lanes((km_rows, LANES), hh), kmean, 0.0))
            x0 = extra0[hh]
            k_fill[pp, hh] = jnp.where(
                klane == x0 + n_blk, 1.0,
                jnp.where(klane == x0 + n_blk + 1, slope * krow, 0.0)).astype(BF16)
            for j in range(n_blk):
                vaug_ref[pp * PAIR + hh, j] = jnp.where(
                    qrow == x0, jnp.ones((), BF16), vt_ref[0, j, _pair_lanes(pp), :])

    def q_with_extras(pp, hh, qb):
        q_pair = qt_ref[0, qb, _pair_lanes(pp), :]
        hi, lo = km_split[pp, hh]
        gate = (_mm(hi, q_pair) + _mm(lo, q_pair))[:n_blk]
        valid = cand < qb
        gate = jnp.where(valid, gate, -jnp.inf)
        rank = jnp.zeros((n_blk, t), jnp.int32)
        for j in range(n_blk):
            other = gate[j:j + 1, :]
            ahead = (other > gate) | ((other == gate) & (j < cand))
            rank = rank + ahead.astype(jnp.int32)
        unsel = jnp.where(valid & (rank < MOBA_TOPK), 0.0, 1.0)
        x0 = extra0[hh]
        pieces = [unsel, jnp.zeros((LANES - x0 - n_blk, t), F32)]
        if x0:
            pieces.insert(0, jnp.zeros((x0, t), F32))
        extras = jnp.concatenate(pieces, axis=0)
        extras = jnp.where(qrow == x0 + n_blk, -slopes[pp, hh] * qcol,
                           jnp.where(qrow == x0 + n_blk + 1, 1.0, extras))
        return jnp.where(_head_rows((LANES, t), hh), q_pair, extras.astype(BF16))

    def k_with_extras(pp, hh, kb, diag):
        k0 = pl.multiple_of(kb * t, t)
        k_pair = k_ref[0, pl.ds(k0, t), _pair_lanes(pp)]
        fill = k_fill[pp, hh]
        if not diag:
            fill = jnp.where(klane == extra0[hh] + kb, jnp.full((), -MASK_BIG, BF16), fill)
        return jnp.where(_head_lanes((t, LANES), hh), k_pair, fill)

    def scores(chains, qs, kb, diags):
        k_x = {phd: k_with_extras(*phd[:2], kb, phd[2])
               for phd in sorted({(c[0], c[1], d) for c, d in zip(chains, diags)})}
        return [_mm(k_x[c[0], c[1], d], q) for c, q, d in zip(chains, qs, diags)]

    def probs(chains, ss, qbs, kb, ms, diags):
        ps, alphas, new_ms = [], [], []
        for (pp, hh, half), s, m, diag in zip(chains, ss, ms, diags):
            if diag:
                s = jnp.where(causal, s, NEG)
                shift = 0.0
            else:
                shift = slopes[pp, hh] * jnp.asarray((qbs[half] - kb) * t, F32)
            m_new = jnp.maximum(m, jnp.max(s, axis=0, keepdims=True) - shift)
            alphas.append(jnp.exp(m - m_new))
            ps.append(jnp.exp(s - (m_new + shift)).astype(BF16))
            new_ms.append(m_new)
        return ps, alphas, new_ms

    chains = _chains()

    def values(ps, alphas, idx, kb, first=None):
        for i, (n, p, alpha) in enumerate(zip(idx, ps, alphas)):
            pv = _mm(vaug_ref[chains[n][0] * PAIR + chains[n][1], kb], p)
            acc_ref[n] = pv if first is not None and first[i] else alpha * acc_ref[n] + pv

    def sup_block(sb, _):
        qbs = [sb * Q_HALVES + i for i in range(Q_HALVES)]
        q = {c: q_with_extras(c[0], c[1], qbs[c[2]]) for c in chains}
        m = {c: jnp.full((1, t), NEG, F32) for c in chains}
        for kk in reversed(range(Q_HALVES)):
            active = [c for c in chains if c[2] >= kk]
            diags = [c[2] == kk for c in active]
            ss = scores(active, [q[c] for c in active], qbs[kk], diags)
            ps, alphas, ms = probs(active, ss, qbs, qbs[kk], [m[c] for c in active], diags)
            m.update(zip(active, ms))
            values(ps, alphas, [chains.index(c) for c in active], qbs[kk], first=diags)

        for pp in range(PAIRS_PER_STEP):
            sub = [c for c in chains if c[0] == pp]
            idx = [chains.index(c) for c in sub]
            qs = [q[c] for c in sub]
            n_sub = len(sub)
            n_now = n_sub - N_DEFER

            def prefetch(kb, sub=sub, qs=qs):
                ss = scores(sub[:N_PREFETCH], qs[:N_PREFETCH], jnp.maximum(kb, 0),
                            [False] * N_PREFETCH)
                for j, s in enumerate(ss):
                    spre_ref[j] = s

            prefetch(qbs[0] - 1)
            for j in range(N_DEFER):
                plate_ref[j] = jnp.zeros((t, t), BF16)

            def body(i, st, sub=sub, idx=idx, qs=qs, n_sub=n_sub, n_now=n_now, prefetch=prefetch):
                kb = qbs[0] - 1 - i
                ms, alphas_late = list(st[:n_sub]), list(st[n_sub:])
                ss = ([spre_ref[j] for j in range(N_PREFETCH)]
                      + scores(sub[N_PREFETCH:], qs[N_PREFETCH:], kb, [False] * (n_sub - N_PREFETCH)))
                prefetch(kb - 1)
                values([plate_ref[j] for j in range(N_DEFER)], alphas_late, idx[n_now:], kb + 1)
                ps, alphas, ms = probs(sub, ss, qbs, kb, ms, [False] * n_sub)
                values(ps[:n_now], alphas[:n_now], idx[:n_now], kb)
                for j in range(N_DEFER):
                    plate_ref[j] = ps[n_now + j]
                return (*ms, *alphas[n_now:])

            st = lax.fori_loop(0, qbs[0], body,
                               (*[m[c] for c in sub], *[jnp.ones((1, t), F32)] * N_DEFER))
            values([plate_ref[j] for j in range(N_DEFER)], st[n_sub:], idx[n_now:], 0)
        for half in range(Q_HALVES):
            for pp in range(PAIRS_PER_STEP):
                outs = []
                for hh in range(PAIR):
                    acc = acc_ref[chains.index((pp, hh, half))]
                    outs.append(acc / acc[extra0[hh]:extra0[hh] + 1, :])
                _store_pair(o_ref, qbs[half], pp, outs)
        return 0

    lax.fori_loop(0, n_sup, sup_block, 0)


def _moba_attention(qt, k, vt, slopes):
    bsz, seq, _ = k.shape
    n_blk = seq // MOBA_BLOCK
    assert ATT_TILE == MOBA_BLOCK and HEAD_DIM % n_blk == 0 and n_blk + 2 <= HEAD_DIM
    call = _att_call(_moba_kernel, "moba", seq, bsz,
                     extra_specs=[pl.BlockSpec(memory_space=pltpu.SMEM)],
                     scratch=[_acc_scratch(),
                              pltpu.VMEM((N_PREFETCH, ATT_TILE, ATT_TILE), F32),
                              pltpu.VMEM((N_DEFER, ATT_TILE, ATT_TILE), BF16),
                              pltpu.VMEM((PAIRS_PER_STEP * PAIR, n_blk, LANES, ATT_TILE), BF16)])
    return call(slopes, qt, k, vt)


def kernel(x, ffn1_norm, ffn1_w_in, ffn1_w_out, mix_norm, w_in, gate_bias, sb_w_out, mb_w_out,
           conv_dw, conv_dw_bias, conv_ln_g, conv_ln_b, conv_w_out, w_o, ffn2_norm, ffn2_w_in,
           ffn2_w_out, final_norm):
    bsz, seq, _ = x.shape
    n = bsz * seq
    depth = w_in.shape[0]
    assert seq % ROW_TILE == 0 and seq % (ATT_TILE * Q_HALVES) == 0 and seq % CONV_TILE == 0
    slopes_np = np.exp2(-8.0 * np.arange(1, N_HEADS + 1, dtype=np.float32) / N_HEADS)
    assert all(math.frexp(float(s))[0] == 0.5 for s in slopes_np) and ATT_TILE <= 256
    slopes = jnp.asarray(slopes_np)

    def vec(a):
        return a.reshape(1, -1)

    def seq3(a):
        return a.reshape(bsz, seq, a.shape[-1])

    h = x.reshape(n, D_MODEL)
    for l in range(depth):
        h = _ffn(h, vec(ffn1_norm[l]), ffn1_w_in[l].astype(BF16), ffn1_w_out[l].astype(BF16))
        w_l = w_in[l].astype(BF16)
        wt = [w_l[:, off:off + ATT_W].T for off in (OFF_SBQ, OFF_SBV, OFF_MBQ, OFF_MBV)]
        qa_t, ka, va_t, qb_t, kb, vb_t, oc, gate = _proj(
            h, vec(mix_norm[l]), w_l, vec(gate_bias[l]), *wt,
            conv_dw[l].reshape(CONV_WIDTH, CONV_CH), vec(conv_dw_bias[l]), vec(conv_ln_g[l]),
            vec(conv_ln_b[l]), bsz, seq)
        oa = _sb_attention(qa_t, seq3(ka), va_t)
        ob = _moba_attention(qb_t, seq3(kb), vb_t, slopes)
        last = l == depth - 1
        h = _mix_ffn(h, oa.reshape(n, ATT_W), ob.reshape(n, ATT_W), oc, gate,
                     sb_w_out[l].astype(BF16), mb_w_out[l].astype(BF16),
                     conv_w_out[l].astype(BF16), w_o[l].astype(BF16), vec(ffn2_norm[l]),
                     ffn2_w_in[l].astype(BF16), ffn2_w_out[l].astype(BF16),
                     vec(final_norm) if last else None)
    return h.reshape(bsz, seq, D_MODEL)
```

```python
import functools
import math

import jax
import jax.numpy as jnp
import numpy as np
from jax import lax
from jax.experimental import pallas as pl
from jax.experimental.pallas import tpu as pltpu

D_MODEL = 1024
HEAD_DIM = 64
N_HEADS = 8
ATT_W = N_HEADS * HEAD_DIM
CONV_CH = D_MODEL // 2
CONV_WIDTH = 31
MOBA_BLOCK = 256
MOBA_TOPK = 3
D_FF = 2816
N_BRANCH = 3
EPS = 1e-6
QK_SCALE = HEAD_DIM ** -0.5

OFF_SBQ = 0
OFF_SBK = OFF_SBQ + ATT_W
OFF_SBV = OFF_SBK + ATT_W
OFF_MBQ = OFF_SBV + ATT_W
OFF_MBK = OFF_MBQ + ATT_W
OFF_MBV = OFF_MBK + ATT_W
OFF_CONV = OFF_MBV + ATT_W
OFF_GATE = OFF_CONV + 2 * CONV_CH
IN_COLS = OFF_GATE + N_BRANCH * D_MODEL

LANES = 128
SUBLANES_BF16 = 16
PAIR = LANES // HEAD_DIM
FF_CHUNK = 256
ROW_TILE = 512
ATT_TILE = 256
Q_HALVES = 2
PAIRS_PER_STEP = 2
SUFFIX_TILE = 128
N_PREFETCH = 2
N_DEFER = 2
CONV_TILE = 128
CONV_HALO = 32
VMEM_LIMIT = 56 * 1024 * 1024
SATURATED = 128.0
MASK_BIG = 2.0 ** 100
NEG = -1e30

F32 = jnp.float32
BF16 = jnp.bfloat16


def _rms(x, g):
    ms = jnp.mean(x * x, axis=-1, keepdims=True)
    return x * lax.rsqrt(ms + EPS) * g


def _sigmoid(x):
    return 0.5 * jnp.tanh(0.5 * x) + 0.5


def _silu(x):
    half = 0.5 * x
    return half * jnp.tanh(half) + half


def _resident(shape):
    nd = len(shape)
    return pl.BlockSpec(shape, lambda *_: (0,) * nd, pipeline_mode=pl.Buffered(1))


def _params(sem):
    return pltpu.CompilerParams(dimension_semantics=sem, vmem_limit_bytes=VMEM_LIMIT)


def _half_step_ffn(x, g_ref, w1_ref, w2_ref, mid_ref):
    xn = _rms(x, g_ref[...]).astype(BF16)
    for c in range(D_FF // FF_CHUNK):
        lo = c * FF_CHUNK
        a = jnp.dot(xn, w1_ref[:, lo:lo + FF_CHUNK], preferred_element_type=F32)
        b = jnp.dot(xn, w1_ref[:, D_FF + lo:D_FF + lo + FF_CHUNK], preferred_element_type=F32)
        mid_ref[:, lo:lo + FF_CHUNK] = (_silu(a) * b).astype(BF16)
    return x + 0.5 * jnp.dot(mid_ref[...], w2_ref[...], preferred_element_type=F32)


def _ffn_kernel(x_ref, g_ref, w1_ref, w2_ref, o_ref, mid_ref):
    o_ref[...] = _half_step_ffn(x_ref[...], g_ref, w1_ref, w2_ref, mid_ref)


def _ffn_specs():
    return [_resident((1, D_MODEL)), _resident((D_MODEL, 2 * D_FF)), _resident((D_FF, D_MODEL))]


def _row(width):
    return pl.BlockSpec((ROW_TILE, width), lambda i: (i, 0))


def _ffn(h, g, w1, w2):
    n = h.shape[0]
    return pl.pallas_call(
        _ffn_kernel,
        grid=(n // ROW_TILE,),
        in_specs=[_row(D_MODEL)] + _ffn_specs(),
        out_specs=_row(D_MODEL),
        out_shape=jax.ShapeDtypeStruct((n, D_MODEL), F32),
        scratch_shapes=[pltpu.VMEM((ROW_TILE, D_FF), BF16)],
        compiler_params=_params(("parallel",)),
        name="ffn",
    )(h, g, w1, w2)


def _mix_ffn_kernel(h_ref, oa_ref, ob_ref, oc_ref, gate_ref, wa_ref, wb_ref, wc_ref, wo_ref,
                    g_ref, w1_ref, w2_ref, *rest, final):
    if final:
        gf_ref, o_ref, mid_ref = rest
    else:
        o_ref, mid_ref = rest
    mixed = None
    for j, (y_ref, w_ref) in enumerate(((oa_ref, wa_ref), (ob_ref, wb_ref), (oc_ref, wc_ref))):
        y = jnp.dot(y_ref[...], w_ref[...], preferred_element_type=F32)
        term = gate_ref[:, j * D_MODEL:(j + 1) * D_MODEL] * y
        mixed = term if mixed is None else mixed + term
    x = h_ref[...] + jnp.dot(mixed.astype(BF16), wo_ref[...], preferred_element_type=F32)
    y = _half_step_ffn(x, g_ref, w1_ref, w2_ref, mid_ref)
    if final:
        y = _rms(y, gf_ref[...])
    o_ref[...] = y


def _mix_ffn(h, oa, ob, oc, gate, wa, wb, wc, wo, g, w1, w2, gf=None):
    n = h.shape[0]
    final = gf is not None
    in_specs = ([_row(D_MODEL), _row(ATT_W), _row(ATT_W), _row(CONV_CH), _row(N_BRANCH * D_MODEL),
                 _resident((ATT_W, D_MODEL)), _resident((ATT_W, D_MODEL)),
                 _resident((CONV_CH, D_MODEL)), _resident((D_MODEL, D_MODEL))] + _ffn_specs())
    args = [h, oa, ob, oc, gate, wa, wb, wc, wo, g, w1, w2]
    if final:
        in_specs.append(_resident((1, D_MODEL)))
        args.append(gf)
    return pl.pallas_call(
        functools.partial(_mix_ffn_kernel, final=final),
        grid=(n // ROW_TILE,),
        in_specs=in_specs,
        out_specs=_row(D_MODEL),
        out_shape=jax.ShapeDtypeStruct((n, D_MODEL), F32),
        scratch_shapes=[pltpu.VMEM((ROW_TILE, D_FF), BF16)],
        compiler_params=_params(("parallel",)),
        name="mix_ffn_final" if final else "mix_ffn",
    )(*args)


def _conv_rows(pad_ref, r0, w_ref, b_ref, g_ref, beta_ref):
    lead = CONV_HALO - (CONV_WIDTH - 1)
    rows = CONV_TILE + CONV_HALO
    chunks = []
    for ch in range(CONV_CH // LANES):
        lanes = slice(ch * LANES, (ch + 1) * LANES)
        x = pad_ref[r0:r0 + rows, lanes]
        acc = jnp.zeros((CONV_TILE, LANES), F32) + b_ref[:, lanes]
        for sub in range(8):
            xs = x if sub == 0 else pltpu.roll(x, rows - sub, axis=0)
            for tap in range(CONV_WIDTH):
                off = tap + lead
                if off % 8 == sub:
                    base = off - sub
                    acc = acc + xs[base:base + CONV_TILE, :] * w_ref[tap:tap + 1, lanes]
        chunks.append(acc)
    acc = jnp.concatenate(chunks, axis=1)
    mu = jnp.mean(acc, axis=-1, keepdims=True)
    d = acc - mu
    var = jnp.mean(d * d, axis=-1, keepdims=True)
    return _silu(d * lax.rsqrt(var + EPS) * g_ref[...] + beta_ref[...]).astype(BF16)


def _proj_kernel(h_ref, g_ref, w_ref, gb_ref, wqa_ref, wva_ref, wqb_ref, wvb_ref,
                 cw_ref, cb_ref, cg_ref, cbeta_ref,
                 qa_ref, ka_ref, va_ref, qb_ref, kb_ref, vb_ref, oc_ref, gate_ref, pad_ref,
                 *, tiles_per_seq):
    u = _rms(h_ref[...], g_ref[...]).astype(BF16)

    def seg(off, width):
        return jnp.dot(u, w_ref[:, off:off + width], preferred_element_type=F32)

    def seg_t(wt_ref, o_ref, scale):
        r = lax.dot_general(wt_ref[...], u, (((1,), (1,)), ((), ())), preferred_element_type=F32)
        r = (r * scale).astype(BF16)
        for j in range(ROW_TILE // ATT_TILE):
            o_ref[0, j] = r[:, j * ATT_TILE:(j + 1) * ATT_TILE]

    @pl.when(pl.program_id(0) % tiles_per_seq == 0)
    def _():
        pad_ref[0:CONV_HALO, :] = jnp.zeros((CONV_HALO, CONV_CH), F32)

    pad_ref[CONV_HALO:, :] = seg(OFF_CONV, CONV_CH) * _sigmoid(seg(OFF_CONV + CONV_CH, CONV_CH))

    for r0 in range(0, ROW_TILE, CONV_TILE):
        oc_ref[r0:r0 + CONV_TILE, :] = _conv_rows(pad_ref, r0, cw_ref, cb_ref, cg_ref, cbeta_ref)
    pad_ref[0:CONV_HALO, :] = pad_ref[ROW_TILE:ROW_TILE + CONV_HALO, :]
    for j in range(N_BRANCH):
        lo = j * D_MODEL
        gate_ref[:, lo:lo + D_MODEL] = _sigmoid(
            seg(OFF_GATE + lo, D_MODEL) + gb_ref[:, lo:lo + D_MODEL]).astype(gate_ref.dtype)
    seg_t(wqa_ref, qa_ref, QK_SCALE)
    seg_t(wva_ref, va_ref, 1.0)
    seg_t(wqb_ref, qb_ref, QK_SCALE)
    seg_t(wvb_ref, vb_ref, 1.0)
    ka_ref[...] = seg(OFF_SBK, ATT_W).astype(BF16)
    kb_ref[...] = seg(OFF_MBK, ATT_W).astype(BF16)


def _proj(h, g, w, gb, wt_qa, wt_va, wt_qb, wt_vb, conv_w, conv_b, conv_g, conv_beta, bsz, seq):
    n = h.shape[0]
    tiles_per_seq = seq // ROW_TILE
    sub = ROW_TILE // ATT_TILE
    feat = pl.BlockSpec((1, sub, ATT_W, ATT_TILE),
                        lambda i: (i // tiles_per_seq, i % tiles_per_seq, 0, 0))
    tok_shape = jax.ShapeDtypeStruct((n, ATT_W), BF16)
    feat_shape = jax.ShapeDtypeStruct((bsz, seq // ATT_TILE, ATT_W, ATT_TILE), BF16)
    wt = _resident((ATT_W, D_MODEL))
    vec = _resident((1, CONV_CH))
    return pl.pallas_call(
        functools.partial(_proj_kernel, tiles_per_seq=tiles_per_seq),
        grid=(n // ROW_TILE,),
        in_specs=[_row(D_MODEL), _resident((1, D_MODEL)), _resident((D_MODEL, IN_COLS)),
                  _resident((1, N_BRANCH * D_MODEL)), wt, wt, wt, wt,
                  _resident((CONV_WIDTH, CONV_CH)), vec, vec, vec],
        out_specs=[feat, _row(ATT_W), feat, feat, _row(ATT_W), feat, _row(CONV_CH),
                   _row(N_BRANCH * D_MODEL)],
        out_shape=[feat_shape, tok_shape, feat_shape, feat_shape, tok_shape, feat_shape,
                   jax.ShapeDtypeStruct((n, CONV_CH), BF16),
                   jax.ShapeDtypeStruct((n, N_BRANCH * D_MODEL), BF16)],
        scratch_shapes=[pltpu.VMEM((CONV_HALO + ROW_TILE, CONV_CH), F32)],
        compiler_params=_params(("arbitrary",)),
        name="proj",
    )(h, g, w, gb, wt_qa, wt_va, wt_qb, wt_vb, conv_w, conv_b, conv_g, conv_beta)


def _head_lanes(shape, hh):
    lane = lax.broadcasted_iota(jnp.int32, shape, len(shape) - 1)
    return (lane >= hh * HEAD_DIM) & (lane < (hh + 1) * HEAD_DIM)


def _head_rows(shape, hh):
    row = lax.broadcasted_iota(jnp.int32, shape, 0)
    return (row >= hh * HEAD_DIM) & (row < (hh + 1) * HEAD_DIM)


def _split_bf16(x):
    hi = x.astype(BF16)
    return hi, (x - hi.astype(F32)).astype(BF16)


def _mm(a, b):
    return jnp.dot(a, b, preferred_element_type=F32)


def _att_call(body, name, seq, bsz, extra_in=(), extra_specs=(), scratch=()):
    n_blk = seq // ATT_TILE
    width = PAIRS_PER_STEP * LANES
    feat = pl.BlockSpec((1, n_blk, width, ATT_TILE), lambda b, p: (b, 0, p, 0))
    tok = pl.BlockSpec((1, seq, width), lambda b, p: (b, 0, p))
    return pl.pallas_call(
        functools.partial(body, seq=seq),
        grid=(bsz, N_HEADS // (PAIR * PAIRS_PER_STEP)),
        in_specs=list(extra_specs) + [feat, tok, feat],
        out_specs=tok,
        out_shape=jax.ShapeDtypeStruct((bsz, seq, ATT_W), BF16),
        scratch_shapes=list(scratch),
        compiler_params=_params(("parallel", "parallel")),
        name=name,
    )


def _acc_scratch():
    return pltpu.VMEM((len(_chains()), LANES, ATT_TILE), F32)


def _chains():
    return [(pp, hh, half) for half in range(Q_HALVES) for pp in range(PAIRS_PER_STEP)
            for hh in range(PAIR)]


def _pair_lanes(pp):
    return slice(pp * LANES, (pp + 1) * LANES)


def _store_pair(o_ref, qb, pp, out_t):
    t = ATT_TILE
    o_t = jnp.where(_head_rows((LANES, t), 0), out_t[0], out_t[1])
    q0 = pl.multiple_of(qb * t, t)
    o_ref[0, pl.ds(q0, t), _pair_lanes(pp)] = o_t.T.astype(o_ref.dtype)


def _sb_kernel(qt_ref, k_ref, vt_ref, o_ref, acc_ref, *, seq):
    t = ATT_TILE
    n_sup = seq // (t * Q_HALVES)
    past = (lax.broadcasted_iota(jnp.int32, (t, t), 0)
            < lax.broadcasted_iota(jnp.int32, (t, t), 1))
    r = lax.broadcasted_iota(jnp.int32, (SUFFIX_TILE, SUFFIX_TILE), 0)
    c = lax.broadcasted_iota(jnp.int32, (SUFFIX_TILE, SUFFIX_TILE), 1)
    upper = (r <= c).astype(BF16)
    upper2 = jnp.concatenate([upper, upper], axis=1)
    n_sub = t // SUFFIX_TILE

    def scores(cs, qs, kbs):
        k_t = {(pp, half): k_ref[0, pl.ds(pl.multiple_of(kbs[half] * t, t), t), _pair_lanes(pp)]
               for pp, half in sorted({(c[0], c[2]) for c in cs})}
        return [_mm(k_t[c[0], c[2]], q) for c, q in zip(cs, qs)]

    def weights(zs, carries, diags):
        stacks = []
        for z, diag in zip(zs, diags):
            sp = jnp.maximum(z, 0.0) + jnp.log(1.0 + jnp.exp(-jnp.abs(z)))
            if diag:
                sp = jnp.where(past, sp, 0.0)
            hi, lo = _split_bf16(sp)
            stacks.append([jnp.concatenate([hi[s * SUFFIX_TILE:(s + 1) * SUFFIX_TILE],
                                            lo[s * SUFFIX_TILE:(s + 1) * SUFFIX_TILE]], axis=0)
                           for s in range(n_sub)])
        incls = [[_mm(upper2, st) for st in stack] for stack in stacks]
        ws, new_carries = [], []
        for z, incl, carry, diag in zip(zs, incls, carries, diags):
            parts = [None] * n_sub
            for s in reversed(range(n_sub)):
                parts[s] = jnp.exp(z[s * SUFFIX_TILE:(s + 1) * SUFFIX_TILE] - incl[s] - carry)
                carry = carry + incl[s][0:1, :]
            w = jnp.concatenate(parts, axis=0)
            ws.append((jnp.where(past, w, 0.0) if diag else w).astype(BF16))
            new_carries.append(carry)
        return ws, new_carries

    chains = _chains()

    def values(ws, idx, kbs, first=False):
        for n, w in zip(idx, ws):
            pp, _, half = chains[n]
            pv = _mm(vt_ref[0, kbs[half], _pair_lanes(pp), :], w)
            acc_ref[n] = pv if first else acc_ref[n] + pv

    def unsaturated(carries):
        return (jnp.min(functools.reduce(jnp.minimum, carries)) < SATURATED).astype(jnp.int32)

    assert Q_HALVES == 2
    last = [n for n, c in enumerate(chains) if c[2] == 1]

    def sup_block(sb, _):
        qbs = [sb * Q_HALVES + i for i in range(Q_HALVES)]
        qs = []
        for pp, hh, half in chains:
            q_pair = qt_ref[0, qbs[half], _pair_lanes(pp), :]
            qs.append(jnp.where(_head_rows((LANES, t), hh), q_pair, jnp.zeros_like(q_pair)))
        every = range(len(chains))
        ws, carries = weights(scores(chains, qs, qbs), [jnp.zeros((1, t), F32)] * len(chains),
                              [True] * len(chains))
        values(ws, every, qbs, first=True)

        def body(st):
            j, carries = st[0], st[2:]
            kbs = [qb - j for qb in qbs]
            ws, carries = weights(scores(chains, qs, kbs), carries, [False] * len(chains))
            values(ws, every, kbs)
            return (j + 1, unsaturated(carries), *carries)

        st = lax.while_loop(lambda st: (st[0] <= qbs[0]) & (st[1] > 0), body,
                            (jnp.int32(1), unsaturated(carries), *carries))

        @pl.when((st[0] == qbs[1]) & (unsaturated([st[2 + n] for n in last]) > 0))
        def _():
            ws, _ = weights(scores([chains[n] for n in last], [qs[n] for n in last], [0, 0]),
                            [st[2 + n] for n in last], [False] * len(last))
            values(ws, last, [0, 0])

        for half in range(Q_HALVES):
            for pp in range(PAIRS_PER_STEP):
                _store_pair(o_ref, qbs[half], pp,
                            [acc_ref[chains.index((pp, hh, half))] for hh in range(PAIR)])
        return 0

    lax.fori_loop(0, n_sup, sup_block, 0)


def _sb_attention(qt, k, vt):
    bsz, seq, _ = k.shape
    return _att_call(_sb_kernel, "stickbreak", seq, bsz, scratch=[_acc_scratch()])(qt, k, vt)


def _moba_kernel(slope_ref, qt_ref, k_ref, vt_ref, o_ref, acc_ref, spre_ref, plate_ref, vaug_ref,
                 *, seq):
    t = ATT_TILE
    n_blk = seq // t
    n_sup = n_blk // Q_HALVES
    km_rows = max(n_blk, SUBLANES_BF16)
    causal = (lax.broadcasted_iota(jnp.int32, (t, t), 0)
              <= lax.broadcasted_iota(jnp.int32, (t, t), 1))
    klane = lax.broadcasted_iota(jnp.int32, (t, LANES), 1)
    krow = lax.broadcasted_iota(jnp.int32, (t, LANES), 0).astype(F32)
    qrow = lax.broadcasted_iota(jnp.int32, (LANES, t), 0)
    qcol = lax.broadcasted_iota(jnp.int32, (LANES, t), 1).astype(F32)
    cand = lax.broadcasted_iota(jnp.int32, (n_blk, t), 0)
    extra0 = [(1 - hh) * HEAD_DIM for hh in range(PAIR)]
    slopes, km_split, k_fill = {}, {}, {}
    for pp in range(PAIRS_PER_STEP):
        kmean = jnp.concatenate(
            [jnp.mean(k_ref[0, j * t:(j + 1) * t, _pair_lanes(pp)].astype(F32), axis=0,
                      keepdims=True) for j in range(n_blk)]
            + ([jnp.zeros((km_rows - n_blk, LANES), F32)] if km_rows > n_blk else []), axis=0)
        for hh in range(PAIR):
            slope = slope_ref[(pl.program_id(1) * PAIRS_PER_STEP + pp) * PAIR + hh]
            slopes[pp, hh] = slope
            km_split[pp, hh] = _split_bf16(
                jnp.where(_head_lanes((km_rows, LANES), hh), kmean, 0.0))
            x0 = extra0[hh]
            k_fill[pp, hh] = jnp.where(
                klane == x0 + n_blk, 1.0,
                jnp.where(klane == x0 + n_blk + 1, slope * krow, 0.0)).astype(BF16)
            for j in range(n_blk):
                vaug_ref[pp * PAIR + hh, j] = jnp.where(
                    qrow == x0, jnp.ones((), BF16), vt_ref[0, j, _pair_lanes(pp), :])

    def q_with_extras(pp, hh, qb):
        q_pair = qt_ref[0, qb, _pair_lanes(pp), :]
        hi, lo = km_split[pp, hh]
        gate = (_mm(hi, q_pair) + _mm(lo, q_pair))[:n_blk]
        valid = cand < qb
        gate = jnp.where(valid, gate, -jnp.inf)
        rank = jnp.zeros((n_blk, t), jnp.int32)
        for j in range(n_blk):
            other = gate[j:j + 1, :]
            ahead = (other > gate) | ((other == gate) & (j < cand))
            rank = rank + ahead.astype(jnp.int32)
        unsel = jnp.where(valid & (rank < MOBA_TOPK), 0.0, 1.0)
        x0 = extra0[hh]
        pieces = [unsel, jnp.zeros((LANES - x0 - n_blk, t), F32)]
        if x0:
            pieces.insert(0, jnp.zeros((x0, t), F32))
        extras = jnp.concatenate(pieces, axis=0)
        extras = jnp.where(qrow == x0 + n_blk, -slopes[pp, hh] * qcol,
                           jnp.where(qrow == x0 + n_blk + 1, 1.0, extras))
        return jnp.where(_head_rows((LANES, t), hh), q_pair, extras.astype(BF16))

    def k_with_extras(pp, hh, kb, diag):
        k0 = pl.multiple_of(kb * t, t)
        k_pair = k_ref[0, pl.ds(k0, t), _pair_lanes(pp)]
        fill = k_fill[pp, hh]
        if not diag:
            fill = jnp.where(klane == extra0[hh] + kb, jnp.full((), -MASK_BIG, BF16), fill)
        return jnp.where(_head_lanes((t, LANES), hh), k_pair, fill)

    def scores(chains, qs, kb, diags):
        k_x = {phd: k_with_extras(*phd[:2], kb, phd[2])
               for phd in sorted({(c[0], c[1], d) for c, d in zip(chains, diags)})}
        return [_mm(k_x[c[0], c[1], d], q) for c, q, d in zip(chains, qs, diags)]

    def probs(chains, ss, qbs, kb, ms, diags):
        ps, alphas, new_ms = [], [], []
        for (pp, hh, half), s, m, diag in zip(chains, ss, ms, diags):
            if diag:
                s = jnp.where(causal, s, NEG)
                shift = 0.0
            else:
                shift = slopes[pp, hh] * jnp.asarray((qbs[half] - kb) * t, F32)
            m_new = jnp.maximum(m, jnp.max(s, axis=0, keepdims=True) - shift)
            alphas.append(jnp.exp(m - m_new))
            ps.append(jnp.exp(s - (m_new + shift)).astype(BF16))
            new_ms.append(m_new)
        return ps, alphas, new_ms

    chains = _chains()

    def values(ps, alphas, idx, kb, first=None):
        for i, (n, p, alpha) in enumerate(zip(idx, ps, alphas)):
            pv = _mm(vaug_ref[chains[n][0] * PAIR + chains[n][1], kb], p)
            acc_ref[n] = pv if first is not None and first[i] else alpha * acc_ref[n] + pv

    def sup_block(sb, _):
        qbs = [sb * Q_HALVES + i for i in range(Q_HALVES)]
        q = {c: q_with_extras(c[0], c[1], qbs[c[2]]) for c in chains}
        m = {c: jnp.full((1, t), NEG, F32) for c in chains}
        for kk in reversed(range(Q_HALVES)):
            active = [c for c in chains if c[2] >= kk]
            diags = [c[2] == kk for c in active]
            ss = scores(active, [q[c] for c in active], qbs[kk], diags)
            ps, alphas, ms = probs(active, ss, qbs, qbs[kk], [m[c] for c in active], diags)
            m.update(zip(active, ms))
            values(ps, alphas, [chains.index(c) for c in active], qbs[kk], first=diags)

        subs = [[c for c in chains if c[0] == pp] for pp in range(PAIRS_PER_STEP)]
        n_sub = len(subs[0])
        n_now = n_sub - N_DEFER

        def prefetch(pp, kb):
            sub = subs[pp][:N_PREFETCH]
            ss = scores(sub, [q[c] for c in sub], jnp.maximum(kb, 0), [False] * N_PREFETCH)
            for j, s in enumerate(ss):
                spre_ref[pp * N_PREFETCH + j] = s

        def late(pp):
            return [plate_ref[pp * N_DEFER + j] for j in range(N_DEFER)]

        for pp in range(PAIRS_PER_STEP):
            prefetch(pp, qbs[0] - 1)
            for j in range(N_DEFER):
                plate_ref[pp * N_DEFER + j] = jnp.zeros((t, t), BF16)
        finals = []
        for pp, sub in enumerate(subs):
            idx = [chains.index(c) for c in sub]
            qs = [q[c] for c in sub]

            def body(i, st, pp=pp, sub=sub, idx=idx, qs=qs):
                kb = qbs[0] - 1 - i
                ms, alphas_late = list(st[:n_sub]), list(st[n_sub:])
                ss = ([spre_ref[pp * N_PREFETCH + j] for j in range(N_PREFETCH)]
                      + scores(sub[N_PREFETCH:], qs[N_PREFETCH:], kb, [False] * (n_sub - N_PREFETCH)))
                prefetch(pp, kb - 1)
                values(late(pp), alphas_late, idx[n_now:], kb + 1)
                ps, alphas, ms = probs(sub, ss, qbs, kb, ms, [False] * n_sub)
                values(ps[:n_now], alphas[:n_now], idx[:n_now], kb)
                for j in range(N_DEFER):
                    plate_ref[pp * N_DEFER + j] = ps[n_now + j]
                return (*ms, *alphas[n_now:])

            st = lax.fori_loop(0, qbs[0], body,
                               (*[m[c] for c in sub], *[jnp.ones((1, t), F32)] * N_DEFER))
            finals.append((idx[n_now:], st[n_sub:]))
        for pp, (idx_late, alphas_late) in enumerate(finals):
            values(late(pp), alphas_late, idx_late, 0)
        for half in range(Q_HALVES):
            for pp in range(PAIRS_PER_STEP):
                outs = []
                for hh in range(PAIR):
                    acc = acc_ref[chains.index((pp, hh, half))]
                    outs.append(acc / acc[extra0[hh]:extra0[hh] + 1, :])
                _store_pair(o_ref, qbs[half], pp, outs)
        return 0

    lax.fori_loop(0, n_sup, sup_block, 0)


def _moba_attention(qt, k, vt, slopes):
    bsz, seq, _ = k.shape
    n_blk = seq // MOBA_BLOCK
    assert ATT_TILE == MOBA_BLOCK and HEAD_DIM % n_blk == 0 and n_blk + 2 <= HEAD_DIM
    call = _att_call(_moba_kernel, "moba", seq, bsz,
                     extra_specs=[pl.BlockSpec(memory_space=pltpu.SMEM)],
                     scratch=[_acc_scratch(),
                              pltpu.VMEM((PAIRS_PER_STEP * N_PREFETCH, ATT_TILE, ATT_TILE), F32),
                              pltpu.VMEM((PAIRS_PER_STEP * N_DEFER, ATT_TILE, ATT_TILE), BF16),
                              pltpu.VMEM((PAIRS_PER_STEP * PAIR, n_blk, LANES, ATT_TILE), BF16)])
    return call(slopes, qt, k, vt)


def kernel(x, ffn1_norm, ffn1_w_in, ffn1_w_out, mix_norm, w_in, gate_bias, sb_w_out, mb_w_out,
           conv_dw, conv_dw_bias, conv_ln_g, conv_ln_b, conv_w_out, w_o, ffn2_norm, ffn2_w_in,
           ffn2_w_out, final_norm):
    bsz, seq, _ = x.shape
    n = bsz * seq
    depth = w_in.shape[0]
    assert seq % ROW_TILE == 0 and seq % (ATT_TILE * Q_HALVES) == 0 and seq % CONV_TILE == 0
    slopes_np = np.exp2(-8.0 * np.arange(1, N_HEADS + 1, dtype=np.float32) / N_HEADS)
    assert all(math.frexp(float(s))[0] == 0.5 for s in slopes_np) and ATT_TILE <= 256
    slopes = jnp.asarray(slopes_np)

    def vec(a):
        return a.reshape(1, -1)

    def seq3(a):
        return a.reshape(bsz, seq, a.shape[-1])

    h = x.reshape(n, D_MODEL)
    for l in range(depth):
        h = _ffn(h, vec(ffn1_norm[l]), ffn1_w_in[l].astype(BF16), ffn1_w_out[l].astype(BF16))
        w_l = w_in[l].astype(BF16)
        wt = [w_l[:, off:off + ATT_W].T for off in (OFF_SBQ, OFF_SBV, OFF_MBQ, OFF_MBV)]
        qa_t, ka, va_t, qb_t, kb, vb_t, oc, gate = _proj(
            h, vec(mix_norm[l]), w_l, vec(gate_bias[l]), *wt,
            conv_dw[l].reshape(CONV_WIDTH, CONV_CH), vec(conv_dw_bias[l]), vec(conv_ln_g[l]),
            vec(conv_ln_b[l]), bsz, seq)
        oa = _sb_attention(qa_t, seq3(ka), va_t)
        ob = _moba_attention(qb_t, seq3(kb), vb_t, slopes)
        last = l == depth - 1
        h = _mix_ffn(h, oa.reshape(n, ATT_W), ob.reshape(n, ATT_W), oc, gate,
                     sb_w_out[l].astype(BF16), mb_w_out[l].astype(BF16),
                     conv_w_out[l].astype(BF16), w_o[l].astype(BF16), vec(ffn2_norm[l]),
                     ffn2_w_in[l].astype(BF16), ffn2_w_out[l].astype(BF16),
                     vec(final_norm) if last else None)
    return h.reshape(bsz, seq, D_MODEL)
```

```python
import functools
import math

import jax
import jax.numpy as jnp
import numpy as np
from jax import lax
from jax.experimental import pallas as pl
from jax.experimental.pallas import tpu as pltpu

D_MODEL = 1024
HEAD_DIM = 64
N_HEADS = 8
ATT_W = N_HEADS * HEAD_DIM
CONV_CH = D_MODEL // 2
CONV_WIDTH = 31
MOBA_BLOCK = 256
MOBA_TOPK = 3
D_FF = 2816
N_BRANCH = 3
EPS = 1e-6
QK_SCALE = HEAD_DIM ** -0.5

OFF_SBQ = 0
OFF_SBK = OFF_SBQ + ATT_W
OFF_SBV = OFF_SBK + ATT_W
OFF_MBQ = OFF_SBV + ATT_W
OFF_MBK = OFF_MBQ + ATT_W
OFF_MBV = OFF_MBK + ATT_W
OFF_CONV = OFF_MBV + ATT_W
OFF_GATE = OFF_CONV + 2 * CONV_CH
IN_COLS = OFF_GATE + N_BRANCH * D_MODEL

LANES = 128
SUBLANES_BF16 = 16
PAIR = LANES // HEAD_DIM
FF_CHUNK = 256
ROW_TILE = 512
ATT_TILE = 256
Q_HALVES = 2
PAIRS_PER_STEP = 4
SUFFIX_TILE = 128
N_PREFETCH = 2
N_DEFER = 2
CONV_TILE = 128
CONV_HALO = 32
VMEM_LIMIT = 56 * 1024 * 1024
SATURATED = 128.0
MASK_BIG = 2.0 ** 100
NEG = -1e30

F32 = jnp.float32
BF16 = jnp.bfloat16


def _rms(x, g):
    ms = jnp.mean(x * x, axis=-1, keepdims=True)
    return x * lax.rsqrt(ms + EPS) * g


def _sigmoid(x):
    return 0.5 * jnp.tanh(0.5 * x) + 0.5


def _silu(x):
    half = 0.5 * x
    return half * jnp.tanh(half) + half


def _resident(shape):
    nd = len(shape)
    return pl.BlockSpec(shape, lambda *_: (0,) * nd, pipeline_mode=pl.Buffered(1))


def _params(sem):
    return pltpu.CompilerParams(dimension_semantics=sem, vmem_limit_bytes=VMEM_LIMIT)


def _half_step_ffn(x, g_ref, w1_ref, w2_ref, mid_ref):
    xn = _rms(x, g_ref[...]).astype(BF16)
    for c in range(D_FF // FF_CHUNK):
        lo = c * FF_CHUNK
        a = jnp.dot(xn, w1_ref[:, lo:lo + FF_CHUNK], preferred_element_type=F32)
        b = jnp.dot(xn, w1_ref[:, D_FF + lo:D_FF + lo + FF_CHUNK], preferred_element_type=F32)
        mid_ref[:, lo:lo + FF_CHUNK] = (_silu(a) * b).astype(BF16)
    return x + 0.5 * jnp.dot(mid_ref[...], w2_ref[...], preferred_element_type=F32)


def _ffn_kernel(x_ref, g_ref, w1_ref, w2_ref, o_ref, mid_ref):
    o_ref[...] = _half_step_ffn(x_ref[...], g_ref, w1_ref, w2_ref, mid_ref)


def _ffn_specs():
    return [_resident((1, D_MODEL)), _resident((D_MODEL, 2 * D_FF)), _resident((D_FF, D_MODEL))]


def _row(width):
    return pl.BlockSpec((ROW_TILE, width), lambda i: (i, 0))


def _ffn(h, g, w1, w2):
    n = h.shape[0]
    return pl.pallas_call(
        _ffn_kernel,
        grid=(n // ROW_TILE,),
        in_specs=[_row(D_MODEL)] + _ffn_specs(),
        out_specs=_row(D_MODEL),
        out_shape=jax.ShapeDtypeStruct((n, D_MODEL), F32),
        scratch_shapes=[pltpu.VMEM((ROW_TILE, D_FF), BF16)],
        compiler_params=_params(("parallel",)),
        name="ffn",
    )(h, g, w1, w2)


def _mix_ffn_kernel(h_ref, oa_ref, ob_ref, oc_ref, gate_ref, wa_ref, wb_ref, wc_ref, wo_ref,
                    g_ref, w1_ref, w2_ref, *rest, final):
    if final:
        gf_ref, o_ref, mid_ref = rest
    else:
        o_ref, mid_ref = rest
    mixed = None
    for j, (y_ref, w_ref) in enumerate(((oa_ref, wa_ref), (ob_ref, wb_ref), (oc_ref, wc_ref))):
        y = jnp.dot(y_ref[...], w_ref[...], preferred_element_type=F32)
        term = gate_ref[:, j * D_MODEL:(j + 1) * D_MODEL] * y
        mixed = term if mixed is None else mixed + term
    x = h_ref[...] + jnp.dot(mixed.astype(BF16), wo_ref[...], preferred_element_type=F32)
    y = _half_step_ffn(x, g_ref, w1_ref, w2_ref, mid_ref)
    if final:
        y = _rms(y, gf_ref[...])
    o_ref[...] = y


def _mix_ffn(h, oa, ob, oc, gate, wa, wb, wc, wo, g, w1, w2, gf=None):
    n = h.shape[0]
    final = gf is not None
    in_specs = ([_row(D_MODEL), _row(ATT_W), _row(ATT_W), _row(CONV_CH), _row(N_BRANCH * D_MODEL),
                 _resident((ATT_W, D_MODEL)), _resident((ATT_W, D_MODEL)),
                 _resident((CONV_CH, D_MODEL)), _resident((D_MODEL, D_MODEL))] + _ffn_specs())
    args = [h, oa, ob, oc, gate, wa, wb, wc, wo, g, w1, w2]
    if final:
        in_specs.append(_resident((1, D_MODEL)))
        args.append(gf)
    return pl.pallas_call(
        functools.partial(_mix_ffn_kernel, final=final),
        grid=(n // ROW_TILE,),
        in_specs=in_specs,
        out_specs=_row(D_MODEL),
        out_shape=jax.ShapeDtypeStruct((n, D_MODEL), F32),
        scratch_shapes=[pltpu.VMEM((ROW_TILE, D_FF), BF16)],
        compiler_params=_params(("parallel",)),
        name="mix_ffn_final" if final else "mix_ffn",
    )(*args)


def _conv_rows(pad_ref, r0, w_ref, b_ref, g_ref, beta_ref):
    lead = CONV_HALO - (CONV_WIDTH - 1)
    rows = CONV_TILE + CONV_HALO
    chunks = []
    for ch in range(CONV_CH // LANES):
        lanes = slice(ch * LANES, (ch + 1) * LANES)
        x = pad_ref[r0:r0 + rows, lanes]
        acc = jnp.zeros((CONV_TILE, LANES), F32) + b_ref[:, lanes]
        for sub in range(8):
            xs = x if sub == 0 else pltpu.roll(x, rows - sub, axis=0)
            for tap in range(CONV_WIDTH):
                off = tap + lead
                if off % 8 == sub:
                    base = off - sub
                    acc = acc + xs[base:base + CONV_TILE, :] * w_ref[tap:tap + 1, lanes]
        chunks.append(acc)
    acc = jnp.concatenate(chunks, axis=1)
    mu = jnp.mean(acc, axis=-1, keepdims=True)
    d = acc - mu
    var = jnp.mean(d * d, axis=-1, keepdims=True)
    return _silu(d * lax.rsqrt(var + EPS) * g_ref[...] + beta_ref[...]).astype(BF16)


def _proj_kernel(h_ref, g_ref, w_ref, gb_ref, wqa_ref, wva_ref, wqb_ref, wvb_ref,
                 cw_ref, cb_ref, cg_ref, cbeta_ref,
                 qa_ref, ka_ref, va_ref, qb_ref, kb_ref, vb_ref, oc_ref, gate_ref, pad_ref,
                 *, tiles_per_seq):
    u = _rms(h_ref[...], g_ref[...]).astype(BF16)

    def seg(off, width):
        return jnp.dot(u, w_ref[:, off:off + width], preferred_element_type=F32)

    def seg_t(wt_ref, o_ref, scale):
        r = lax.dot_general(wt_ref[...], u, (((1,), (1,)), ((), ())), preferred_element_type=F32)
        r = (r * scale).astype(BF16)
        for j in range(ROW_TILE // ATT_TILE):
            o_ref[0, j] = r[:, j * ATT_TILE:(j + 1) * ATT_TILE]

    @pl.when(pl.program_id(0) % tiles_per_seq == 0)
    def _():
        pad_ref[0:CONV_HALO, :] = jnp.zeros((CONV_HALO, CONV_CH), F32)

    pad_ref[CONV_HALO:, :] = seg(OFF_CONV, CONV_CH) * _sigmoid(seg(OFF_CONV + CONV_CH, CONV_CH))

    for r0 in range(0, ROW_TILE, CONV_TILE):
        oc_ref[r0:r0 + CONV_TILE, :] = _conv_rows(pad_ref, r0, cw_ref, cb_ref, cg_ref, cbeta_ref)
    pad_ref[0:CONV_HALO, :] = pad_ref[ROW_TILE:ROW_TILE + CONV_HALO, :]
    for j in range(N_BRANCH):
        lo = j * D_MODEL
        gate_ref[:, lo:lo + D_MODEL] = _sigmoid(
            seg(OFF_GATE + lo, D_MODEL) + gb_ref[:, lo:lo + D_MODEL]).astype(gate_ref.dtype)
    seg_t(wqa_ref, qa_ref, QK_SCALE)
    seg_t(wva_ref, va_ref, 1.0)
    seg_t(wqb_ref, qb_ref, QK_SCALE)
    seg_t(wvb_ref, vb_ref, 1.0)
    ka_ref[...] = seg(OFF_SBK, ATT_W).astype(BF16)
    kb_ref[...] = seg(OFF_MBK, ATT_W).astype(BF16)


def _proj(h, g, w, gb, wt_qa, wt_va, wt_qb, wt_vb, conv_w, conv_b, conv_g, conv_beta, bsz, seq):
    n = h.shape[0]
    tiles_per_seq = seq // ROW_TILE
    sub = ROW_TILE // ATT_TILE
    feat = pl.BlockSpec((1, sub, ATT_W, ATT_TILE),
                        lambda i: (i // tiles_per_seq, i % tiles_per_seq, 0, 0))
    tok_shape = jax.ShapeDtypeStruct((n, ATT_W), BF16)
    feat_shape = jax.ShapeDtypeStruct((bsz, seq // ATT_TILE, ATT_W, ATT_TILE), BF16)
    wt = _resident((ATT_W, D_MODEL))
    vec = _resident((1, CONV_CH))
    return pl.pallas_call(
        functools.partial(_proj_kernel, tiles_per_seq=tiles_per_seq),
        grid=(n // ROW_TILE,),
        in_specs=[_row(D_MODEL), _resident((1, D_MODEL)), _resident((D_MODEL, IN_COLS)),
                  _resident((1, N_BRANCH * D_MODEL)), wt, wt, wt, wt,
                  _resident((CONV_WIDTH, CONV_CH)), vec, vec, vec],
        out_specs=[feat, _row(ATT_W), feat, feat, _row(ATT_W), feat, _row(CONV_CH),
                   _row(N_BRANCH * D_MODEL)],
        out_shape=[feat_shape, tok_shape, feat_shape, feat_shape, tok_shape, feat_shape,
                   jax.ShapeDtypeStruct((n, CONV_CH), BF16),
                   jax.ShapeDtypeStruct((n, N_BRANCH * D_MODEL), BF16)],
        scratch_shapes=[pltpu.VMEM((CONV_HALO + ROW_TILE, CONV_CH), F32)],
        compiler_params=_params(("arbitrary",)),
        name="proj",
    )(h, g, w, gb, wt_qa, wt_va, wt_qb, wt_vb, conv_w, conv_b, conv_g, conv_beta)


def _head_lanes(shape, hh):
    lane = lax.broadcasted_iota(jnp.int32, shape, len(shape) - 1)
    return (lane >= hh * HEAD_DIM) & (lane < (hh + 1) * HEAD_DIM)


def _head_rows(shape, hh):
    row = lax.broadcasted_iota(jnp.int32, shape, 0)
    return (row >= hh * HEAD_DIM) & (row < (hh + 1) * HEAD_DIM)


def _split_bf16(x):
    hi = x.astype(BF16)
    return hi, (x - hi.astype(F32)).astype(BF16)


def _mm(a, b):
    return jnp.dot(a, b, preferred_element_type=F32)


def _att_call(body, name, seq, bsz, extra_in=(), extra_specs=(), scratch=()):
    n_blk = seq // ATT_TILE
    width = PAIRS_PER_STEP * LANES
    feat = pl.BlockSpec((1, n_blk, width, ATT_TILE), lambda b, p: (b, 0, p, 0))
    tok = pl.BlockSpec((1, seq, width), lambda b, p: (b, 0, p))
    return pl.pallas_call(
        functools.partial(body, seq=seq),
        grid=(bsz, N_HEADS // (PAIR * PAIRS_PER_STEP)),
        in_specs=list(extra_specs) + [feat, tok, feat],
        out_specs=tok,
        out_shape=jax.ShapeDtypeStruct((bsz, seq, ATT_W), BF16),
        scratch_shapes=list(scratch),
        compiler_params=_params(("parallel", "parallel")),
        name=name,
    )


def _acc_scratch():
    return pltpu.VMEM((len(_chains()), LANES, ATT_TILE), F32)


def _chains():
    return [(pp, hh, half) for half in range(Q_HALVES) for pp in range(PAIRS_PER_STEP)
            for hh in range(PAIR)]


def _pair_lanes(pp):
    return slice(pp * LANES, (pp + 1) * LANES)


def _store_pair(o_ref, qb, pp, out_t):
    t = ATT_TILE
    o_t = jnp.where(_head_rows((LANES, t), 0), out_t[0], out_t[1])
    q0 = pl.multiple_of(qb * t, t)
    o_ref[0, pl.ds(q0, t), _pair_lanes(pp)] = o_t.T.astype(o_ref.dtype)


def _sb_kernel(qt_ref, k_ref, vt_ref, o_ref, acc_ref, *, seq):
    t = ATT_TILE
    n_sup = seq // (t * Q_HALVES)
    past = (lax.broadcasted_iota(jnp.int32, (t, t), 0)
            < lax.broadcasted_iota(jnp.int32, (t, t), 1))
    r = lax.broadcasted_iota(jnp.int32, (SUFFIX_TILE, SUFFIX_TILE), 0)
    c = lax.broadcasted_iota(jnp.int32, (SUFFIX_TILE, SUFFIX_TILE), 1)
    upper = (r <= c).astype(BF16)
    upper2 = jnp.concatenate([upper, upper], axis=1)
    n_sub = t // SUFFIX_TILE

    def scores(cs, qs, kbs):
        k_t = {(pp, half): k_ref[0, pl.ds(pl.multiple_of(kbs[half] * t, t), t), _pair_lanes(pp)]
               for pp, half in sorted({(c[0], c[2]) for c in cs})}
        return [_mm(k_t[c[0], c[2]], q) for c, q in zip(cs, qs)]

    def weights(zs, carries, diags):
        stacks = []
        for z, diag in zip(zs, diags):
            sp = jnp.maximum(z, 0.0) + jnp.log(1.0 + jnp.exp(-jnp.abs(z)))
            if diag:
                sp = jnp.where(past, sp, 0.0)
            hi, lo = _split_bf16(sp)
            stacks.append([jnp.concatenate([hi[s * SUFFIX_TILE:(s + 1) * SUFFIX_TILE],
                                            lo[s * SUFFIX_TILE:(s + 1) * SUFFIX_TILE]], axis=0)
                           for s in range(n_sub)])
        incls = [[_mm(upper2, st) for st in stack] for stack in stacks]
        ws, new_carries = [], []
        for z, incl, carry, diag in zip(zs, incls, carries, diags):
            parts = [None] * n_sub
            for s in reversed(range(n_sub)):
                parts[s] = jnp.exp(z[s * SUFFIX_TILE:(s + 1) * SUFFIX_TILE] - incl[s] - carry)
                carry = carry + incl[s][0:1, :]
            w = jnp.concatenate(parts, axis=0)
            ws.append((jnp.where(past, w, 0.0) if diag else w).astype(BF16))
            new_carries.append(carry)
        return ws, new_carries

    chains = _chains()

    def values(ws, idx, kbs, first=False):
        for n, w in zip(idx, ws):
            pp, _, half = chains[n]
            pv = _mm(vt_ref[0, kbs[half], _pair_lanes(pp), :], w)
            acc_ref[n] = pv if first else acc_ref[n] + pv

    def unsaturated(carries):
        return (jnp.min(functools.reduce(jnp.minimum, carries)) < SATURATED).astype(jnp.int32)

    assert Q_HALVES == 2
    last = [n for n, c in enumerate(chains) if c[2] == 1]

    def sup_block(sb, _):
        qbs = [sb * Q_HALVES + i for i in range(Q_HALVES)]
        qs = []
        for pp, hh, half in chains:
            q_pair = qt_ref[0, qbs[half], _pair_lanes(pp), :]
            qs.append(jnp.where(_head_rows((LANES, t), hh), q_pair, jnp.zeros_like(q_pair)))
        every = range(len(chains))
        ws, carries = weights(scores(chains, qs, qbs), [jnp.zeros((1, t), F32)] * len(chains),
                              [True] * len(chains))
        values(ws, every, qbs, first=True)

        def body(st):
            j, carries = st[0], st[2:]
            kbs = [qb - j for qb in qbs]
            ws, carries = weights(scores(chains, qs, kbs), carries, [False] * len(chains))
            values(ws, every, kbs)
            return (j + 1, unsaturated(carries), *carries)

        st = lax.while_loop(lambda st: (st[0] <= qbs[0]) & (st[1] > 0), body,
                            (jnp.int32(1), unsaturated(carries), *carries))

        @pl.when((st[0] == qbs[1]) & (unsaturated([st[2 + n] for n in last]) > 0))
        def _():
            ws, _ = weights(scores([chains[n] for n in last], [qs[n] for n in last], [0, 0]),
                            [st[2 + n] for n in last], [False] * len(last))
            values(ws, last, [0, 0])

        for half in range(Q_HALVES):
            for pp in range(PAIRS_PER_STEP):
                _store_pair(o_ref, qbs[half], pp,
                            [acc_ref[chains.index((pp, hh, half))] for hh in range(PAIR)])
        return 0

    lax.fori_loop(0, n_sup, sup_block, 0)


def _sb_attention(qt, k, vt):
    bsz, seq, _ = k.shape
    return _att_call(_sb_kernel, "stickbreak", seq, bsz, scratch=[_acc_scratch()])(qt, k, vt)


def _moba_kernel(slope_ref, qt_ref, k_ref, vt_ref, o_ref, acc_ref, spre_ref, plate_ref, vaug_ref,
                 *, seq):
    t = ATT_TILE
    n_blk = seq // t
    n_sup = n_blk // Q_HALVES
    km_rows = max(n_blk, SUBLANES_BF16)
    causal = (lax.broadcasted_iota(jnp.int32, (t, t), 0)
              <= lax.broadcasted_iota(jnp.int32, (t, t), 1))
    klane = lax.broadcasted_iota(jnp.int32, (t, LANES), 1)
    krow = lax.broadcasted_iota(jnp.int32, (t, LANES), 0).astype(F32)
    qrow = lax.broadcasted_iota(jnp.int32, (LANES, t), 0)
    qcol = lax.broadcasted_iota(jnp.int32, (LANES, t), 1).astype(F32)
    cand = lax.broadcasted_iota(jnp.int32, (n_blk, t), 0)
    extra0 = [(1 - hh) * HEAD_DIM for hh in range(PAIR)]
    slopes, km_split, k_fill = {}, {}, {}
    for pp in range(PAIRS_PER_STEP):
        kmean = jnp.concatenate(
            [jnp.mean(k_ref[0, j * t:(j + 1) * t, _pair_lanes(pp)].astype(F32), axis=0,
                      keepdims=True) for j in range(n_blk)]
            + ([jnp.zeros((km_rows - n_blk, LANES), F32)] if km_rows > n_blk else []), axis=0)
        for hh in range(PAIR):
            slope = slope_ref[(pl.program_id(1) * PAIRS_PER_STEP + pp) * PAIR + hh]
            slopes[pp, hh] = slope
            km_split[pp, hh] = _split_bf16(
                jnp.where(_head_lanes((km_rows, LANES), hh), kmean, 0.0))
            x0 = extra0[hh]
            k_fill[pp, hh] = jnp.where(
                klane == x0 + n_blk, 1.0,
                jnp.where(klane == x0 + n_blk + 1, slope * krow, 0.0)).astype(BF16)
            for j in range(n_blk):
                vaug_ref[pp * PAIR + hh, j] = jnp.where(
                    qrow == x0, jnp.ones((), BF16), vt_ref[0, j, _pair_lanes(pp), :])

    def q_with_extras(pp, hh, qb):
        q_pair = qt_ref[0, qb, _pair_lanes(pp), :]
        hi, lo = km_split[pp, hh]
        gate = (_mm(hi, q_pair) + _mm(lo, q_pair))[:n_blk]
        valid = cand < qb
        gate = jnp.where(valid, gate, -jnp.inf)
        rank = jnp.zeros((n_blk, t), jnp.int32)
        for j in range(n_blk):
            other = gate[j:j + 1, :]
            ahead = (other > gate) | ((other == gate) & (j < cand))
            rank = rank + ahead.astype(jnp.int32)
        unsel = jnp.where(valid & (rank < MOBA_TOPK), 0.0, 1.0)
        x0 = extra0[hh]
        pieces = [unsel, jnp.zeros((LANES - x0 - n_blk, t), F32)]
        if x0:
            pieces.insert(0, jnp.zeros((x0, t), F32))
        extras = jnp.concatenate(pieces, axis=0)
        extras = jnp.where(qrow == x0 + n_blk, -slopes[pp, hh] * qcol,
                           jnp.where(qrow == x0 + n_blk + 1, 1.0, extras))
        return jnp.where(_head_rows((LANES, t), hh), q_pair, extras.astype(BF16))

    def k_with_extras(pp, hh, kb, diag):
        k0 = pl.multiple_of(kb * t, t)
        k_pair = k_ref[0, pl.ds(k0, t), _pair_lanes(pp)]
        fill = k_fill[pp, hh]
        if not diag:
            fill = jnp.where(klane == extra0[hh] + kb, jnp.full((), -MASK_BIG, BF16), fill)
        return jnp.where(_head_lanes((t, LANES), hh), k_pair, fill)

    def scores(chains, qs, kb, diags):
        k_x = {phd: k_with_extras(*phd[:2], kb, phd[2])
               for phd in sorted({(c[0], c[1], d) for c, d in zip(chains, diags)})}
        return [_mm(k_x[c[0], c[1], d], q) for c, q, d in zip(chains, qs, diags)]

    def probs(chains, ss, qbs, kb, ms, diags):
        ps, alphas, new_ms = [], [], []
        for (pp, hh, half), s, m, diag in zip(chains, ss, ms, diags):
            if diag:
                s = jnp.where(causal, s, NEG)
                shift = 0.0
            else:
                shift = slopes[pp, hh] * jnp.asarray((qbs[half] - kb) * t, F32)
            m_new = jnp.maximum(m, jnp.max(s, axis=0, keepdims=True) - shift)
            alphas.append(jnp.exp(m - m_new))
            ps.append(jnp.exp(s - (m_new + shift)).astype(BF16))
            new_ms.append(m_new)
        return ps, alphas, new_ms

    chains = _chains()

    def values(ps, alphas, idx, kb, first=None):
        for i, (n, p, alpha) in enumerate(zip(idx, ps, alphas)):
            pv = _mm(vaug_ref[chains[n][0] * PAIR + chains[n][1], kb], p)
            acc_ref[n] = pv if first is not None and first[i] else alpha * acc_ref[n] + pv

    def sup_block(sb, _):
        qbs = [sb * Q_HALVES + i for i in range(Q_HALVES)]
        q = {c: q_with_extras(c[0], c[1], qbs[c[2]]) for c in chains}
        m = {c: jnp.full((1, t), NEG, F32) for c in chains}
        for kk in reversed(range(Q_HALVES)):
            active = [c for c in chains if c[2] >= kk]
            diags = [c[2] == kk for c in active]
            ss = scores(active, [q[c] for c in active], qbs[kk], diags)
            ps, alphas, ms = probs(active, ss, qbs, qbs[kk], [m[c] for c in active], diags)
            m.update(zip(active, ms))
            values(ps, alphas, [chains.index(c) for c in active], qbs[kk], first=diags)

        subs = [[c for c in chains if c[0] == pp] for pp in range(PAIRS_PER_STEP)]
        n_sub = len(subs[0])
        n_now = n_sub - N_DEFER

        def prefetch(pp, kb):
            sub = subs[pp][:N_PREFETCH]
            ss = scores(sub, [q[c] for c in sub], jnp.maximum(kb, 0), [False] * N_PREFETCH)
            for j, s in enumerate(ss):
                spre_ref[pp * N_PREFETCH + j] = s

        def late(pp):
            return [plate_ref[pp * N_DEFER + j] for j in range(N_DEFER)]

        for pp in range(PAIRS_PER_STEP):
            prefetch(pp, qbs[0] - 1)
            for j in range(N_DEFER):
                plate_ref[pp * N_DEFER + j] = jnp.zeros((t, t), BF16)
        finals = []
        for pp, sub in enumerate(subs):
            idx = [chains.index(c) for c in sub]
            qs = [q[c] for c in sub]

            def body(i, st, pp=pp, sub=sub, idx=idx, qs=qs):
                kb = qbs[0] - 1 - i
                ms, alphas_late = list(st[:n_sub]), list(st[n_sub:])
                ss = ([spre_ref[pp * N_PREFETCH + j] for j in range(N_PREFETCH)]
                      + scores(sub[N_PREFETCH:], qs[N_PREFETCH:], kb, [False] * (n_sub - N_PREFETCH)))
                prefetch(pp, kb - 1)
                values(late(pp), alphas_late, idx[n_now:], kb + 1)
                ps, alphas, ms = probs(sub, ss, qbs, kb, ms, [False] * n_sub)
                values(ps[:n_now], alphas[:n_now], idx[:n_now], kb)
                for j in range(N_DEFER):
                    plate_ref[pp * N_DEFER + j] = ps[n_now + j]
                return (*ms, *alphas[n_now:])

            st = lax.fori_loop(0, qbs[0], body,
                               (*[m[c] for c in sub], *[jnp.ones((1, t), F32)] * N_DEFER))
            finals.append((idx[n_now:], st[n_sub:]))
        for pp, (idx_late, alphas_late) in enumerate(finals):
            values(late(pp), alphas_late, idx_late, 0)
        for half in range(Q_HALVES):
            for pp in range(PAIRS_PER_STEP):
                outs = []
                for hh in range(PAIR):
                    acc = acc_ref[chains.index((pp, hh, half))]
                    outs.append(acc / acc[extra0[hh]:extra0[hh] + 1, :])
                _store_pair(o_ref, qbs[half], pp, outs)
        return 0

    lax.fori_loop(0, n_sup, sup_block, 0)


def _moba_attention(qt, k, vt, slopes):
    bsz, seq, _ = k.shape
    n_blk = seq // MOBA_BLOCK
    assert ATT_TILE == MOBA_BLOCK and HEAD_DIM % n_blk == 0 and n_blk + 2 <= HEAD_DIM
    call = _att_call(_moba_kernel, "moba", seq, bsz,
                     extra_specs=[pl.BlockSpec(memory_space=pltpu.SMEM)],
                     scratch=[_acc_scratch(),
                              pltpu.VMEM((PAIRS_PER_STEP * N_PREFETCH, ATT_TILE, ATT_TILE), F32),
                              pltpu.VMEM((PAIRS_PER_STEP * N_DEFER, ATT_TILE, ATT_TILE), BF16),
                              pltpu.VMEM((PAIRS_PER_STEP * PAIR, n_blk, LANES, ATT_TILE), BF16)])
    return call(slopes, qt, k, vt)


def kernel(x, ffn1_norm, ffn1_w_in, ffn1_w_out, mix_norm, w_in, gate_bias, sb_w_out, mb_w_out,
           conv_dw, conv_dw_bias, conv_ln_g, conv_ln_b, conv_w_out, w_o, ffn2_norm, ffn2_w_in,
           ffn2_w_out, final_norm):
    bsz, seq, _ = x.shape
    n = bsz * seq
    depth = w_in.shape[0]
    assert seq % ROW_TILE == 0 and seq % (ATT_TILE * Q_HALVES) == 0 and seq % CONV_TILE == 0
    slopes_np = np.exp2(-8.0 * np.arange(1, N_HEADS + 1, dtype=np.float32) / N_HEADS)
    assert all(math.frexp(float(s))[0] == 0.5 for s in slopes_np) and ATT_TILE <= 256
    slopes = jnp.asarray(slopes_np)

    def vec(a):
        return a.reshape(1, -1)

    def seq3(a):
        return a.reshape(bsz, seq, a.shape[-1])

    h = x.reshape(n, D_MODEL)
    for l in range(depth):
        h = _ffn(h, vec(ffn1_norm[l]), ffn1_w_in[l].astype(BF16), ffn1_w_out[l].astype(BF16))
        w_l = w_in[l].astype(BF16)
        wt = [w_l[:, off:off + ATT_W].T for off in (OFF_SBQ, OFF_SBV, OFF_MBQ, OFF_MBV)]
        qa_t, ka, va_t, qb_t, kb, vb_t, oc, gate = _proj(
            h, vec(mix_norm[l]), w_l, vec(gate_bias[l]), *wt,
            conv_dw[l].reshape(CONV_WIDTH, CONV_CH), vec(conv_dw_bias[l]), vec(conv_ln_g[l]),
            vec(conv_ln_b[l]), bsz, seq)
        oa = _sb_attention(qa_t, seq3(ka), va_t)
        ob = _moba_attention(qb_t, seq3(kb), vb_t, slopes)
        last = l == depth - 1
        h = _mix_ffn(h, oa.reshape(n, ATT_W), ob.reshape(n, ATT_W), oc, gate,
                     sb_w_out[l].astype(BF16), mb_w_out[l].astype(BF16),
                     conv_w_out[l].astype(BF16), w_o[l].astype(BF16), vec(ffn2_norm[l]),
                     ffn2_w_in[l].astype(BF16), ffn2_w_out[l].astype(BF16),
                     vec(final_norm) if last else None)
    return h.reshape(bsz, seq, D_MODEL)
```

```python
import functools
import math

import jax
import jax.numpy as jnp
import numpy as np
from jax import lax
from jax.experimental import pallas as pl
from jax.experimental.pallas import tpu as pltpu

D_MODEL = 1024
HEAD_DIM = 64
N_HEADS = 8
ATT_W = N_HEADS * HEAD_DIM
CONV_CH = D_MODEL // 2
CONV_WIDTH = 31
MOBA_BLOCK = 256
MOBA_TOPK = 3
D_FF = 2816
N_BRANCH = 3
EPS = 1e-6
QK_SCALE = HEAD_DIM ** -0.5

OFF_SBQ = 0
OFF_SBK = OFF_SBQ + ATT_W
OFF_SBV = OFF_SBK + ATT_W
OFF_MBQ = OFF_SBV + ATT_W
OFF_MBK = OFF_MBQ + ATT_W
OFF_MBV = OFF_MBK + ATT_W
OFF_CONV = OFF_MBV + ATT_W
OFF_GATE = OFF_CONV + 2 * CONV_CH
IN_COLS = OFF_GATE + N_BRANCH * D_MODEL

LANES = 128
SUBLANES_BF16 = 16
PAIR = LANES // HEAD_DIM
FF_CHUNK = 256
ROW_TILE = 512
ATT_TILE = 256
Q_HALVES = 2
PAIRS_PER_STEP = 4
SUFFIX_TILE = 128
N_PREFETCH = 2
N_DEFER = 2
CONV_TILE = 128
CONV_HALO = 32
VMEM_LIMIT = 56 * 1024 * 1024
SATURATED = 128.0
MASK_BIG = 2.0 ** 100
NEG = -1e30

F32 = jnp.float32
BF16 = jnp.bfloat16


def _rms(x, g):
    ms = jnp.mean(x * x, axis=-1, keepdims=True)
    return x * lax.rsqrt(ms + EPS) * g


def _sigmoid(x):
    return 0.5 * jnp.tanh(0.5 * x) + 0.5


def _silu(x):
    half = 0.5 * x
    return half * jnp.tanh(half) + half


def _resident(shape):
    nd = len(shape)
    return pl.BlockSpec(shape, lambda *_: (0,) * nd, pipeline_mode=pl.Buffered(1))


def _params(sem):
    return pltpu.CompilerParams(dimension_semantics=sem, vmem_limit_bytes=VMEM_LIMIT)


def _half_step_ffn(x, g_ref, w1_ref, w2_ref, mid_ref):
    xn = _rms(x, g_ref[...]).astype(BF16)
    for c in range(D_FF // FF_CHUNK):
        lo = c * FF_CHUNK
        a = jnp.dot(xn, w1_ref[:, lo:lo + FF_CHUNK], preferred_element_type=F32)
        b = jnp.dot(xn, w1_ref[:, D_FF + lo:D_FF + lo + FF_CHUNK], preferred_element_type=F32)
        mid_ref[:, lo:lo + FF_CHUNK] = (_silu(a) * b).astype(BF16)
    return x + 0.5 * jnp.dot(mid_ref[...], w2_ref[...], preferred_element_type=F32)


def _ffn_kernel(x_ref, g_ref, w1_ref, w2_ref, o_ref, mid_ref):
    o_ref[...] = _half_step_ffn(x_ref[...], g_ref, w1_ref, w2_ref, mid_ref)


def _ffn_specs():
    return [_resident((1, D_MODEL)), _resident((D_MODEL, 2 * D_FF)), _resident((D_FF, D_MODEL))]


def _row(width):
    return pl.BlockSpec((ROW_TILE, width), lambda i: (i, 0))


def _ffn(h, g, w1, w2):
    n = h.shape[0]
    return pl.pallas_call(
        _ffn_kernel,
        grid=(n // ROW_TILE,),
        in_specs=[_row(D_MODEL)] + _ffn_specs(),
        out_specs=_row(D_MODEL),
        out_shape=jax.ShapeDtypeStruct((n, D_MODEL), F32),
        scratch_shapes=[pltpu.VMEM((ROW_TILE, D_FF), BF16)],
        compiler_params=_params(("parallel",)),
        name="ffn",
    )(h, g, w1, w2)


def _mix_ffn_kernel(h_ref, oa_ref, ob_ref, oc_ref, gate_ref, wa_ref, wb_ref, wc_ref, wo_ref,
                    g_ref, w1_ref, w2_ref, *rest, final):
    if final:
        gf_ref, o_ref, mid_ref = rest
    else:
        o_ref, mid_ref = rest
    mixed = None
    for j, (y_ref, w_ref) in enumerate(((oa_ref, wa_ref), (ob_ref, wb_ref), (oc_ref, wc_ref))):
        y = jnp.dot(y_ref[...], w_ref[...], preferred_element_type=F32)
        term = gate_ref[:, j * D_MODEL:(j + 1) * D_MODEL] * y
        mixed = term if mixed is None else mixed + term
    x = h_ref[...] + jnp.dot(mixed.astype(BF16), wo_ref[...], preferred_element_type=F32)
    y = _half_step_ffn(x, g_ref, w1_ref, w2_ref, mid_ref)
    if final:
        y = _rms(y, gf_ref[...])
    o_ref[...] = y


def _mix_ffn(h, oa, ob, oc, gate, wa, wb, wc, wo, g, w1, w2, gf=None):
    n = h.shape[0]
    final = gf is not None
    in_specs = ([_row(D_MODEL), _row(ATT_W), _row(ATT_W), _row(CONV_CH), _row(N_BRANCH * D_MODEL),
                 _resident((ATT_W, D_MODEL)), _resident((ATT_W, D_MODEL)),
                 _resident((CONV_CH, D_MODEL)), _resident((D_MODEL, D_MODEL))] + _ffn_specs())
    args = [h, oa, ob, oc, gate, wa, wb, wc, wo, g, w1, w2]
    if final:
        in_specs.append(_resident((1, D_MODEL)))
        args.append(gf)
    return pl.pallas_call(
        functools.partial(_mix_ffn_kernel, final=final),
        grid=(n // ROW_TILE,),
        in_specs=in_specs,
        out_specs=_row(D_MODEL),
        out_shape=jax.ShapeDtypeStruct((n, D_MODEL), F32),
        scratch_shapes=[pltpu.VMEM((ROW_TILE, D_FF), BF16)],
        compiler_params=_params(("parallel",)),
        name="mix_ffn_final" if final else "mix_ffn",
    )(*args)


def _conv_rows(pad_ref, r0, w_ref, b_ref, g_ref, beta_ref):
    lead = CONV_HALO - (CONV_WIDTH - 1)
    rows = CONV_TILE + CONV_HALO
    chunks = []
    for ch in range(CONV_CH // LANES):
        lanes = slice(ch * LANES, (ch + 1) * LANES)
        x = pad_ref[r0:r0 + rows, lanes]
        acc = jnp.zeros((CONV_TILE, LANES), F32) + b_ref[:, lanes]
        for sub in range(8):
            xs = x if sub == 0 else pltpu.roll(x, rows - sub, axis=0)
            for tap in range(CONV_WIDTH):
                off = tap + lead
                if off % 8 == sub:
                    base = off - sub
                    acc = acc + xs[base:base + CONV_TILE, :] * w_ref[tap:tap + 1, lanes]
        chunks.append(acc)
    acc = jnp.concatenate(chunks, axis=1)
    mu = jnp.mean(acc, axis=-1, keepdims=True)
    d = acc - mu
    var = jnp.mean(d * d, axis=-1, keepdims=True)
    return _silu(d * lax.rsqrt(var + EPS) * g_ref[...] + beta_ref[...]).astype(BF16)


def _proj_kernel(h_ref, g_ref, w_ref, gb_ref, wqa_ref, wva_ref, wqb_ref, wvb_ref,
                 cw_ref, cb_ref, cg_ref, cbeta_ref,
                 qa_ref, ka_ref, va_ref, qb_ref, kb_ref, vb_ref, oc_ref, gate_ref, pad_ref,
                 *, tiles_per_seq):
    u = _rms(h_ref[...], g_ref[...]).astype(BF16)

    def seg(off, width):
        return jnp.dot(u, w_ref[:, off:off + width], preferred_element_type=F32)

    def seg_t(wt_ref, o_ref, scale):
        r = lax.dot_general(wt_ref[...], u, (((1,), (1,)), ((), ())), preferred_element_type=F32)
        r = (r * scale).astype(BF16)
        for j in range(ROW_TILE // ATT_TILE):
            o_ref[0, j] = r[:, j * ATT_TILE:(j + 1) * ATT_TILE]

    @pl.when(pl.program_id(0) % tiles_per_seq == 0)
    def _():
        pad_ref[0:CONV_HALO, :] = jnp.zeros((CONV_HALO, CONV_CH), F32)

    pad_ref[CONV_HALO:, :] = seg(OFF_CONV, CONV_CH) * _sigmoid(seg(OFF_CONV + CONV_CH, CONV_CH))

    for r0 in range(0, ROW_TILE, CONV_TILE):
        oc_ref[r0:r0 + CONV_TILE, :] = _conv_rows(pad_ref, r0, cw_ref, cb_ref, cg_ref, cbeta_ref)
    pad_ref[0:CONV_HALO, :] = pad_ref[ROW_TILE:ROW_TILE + CONV_HALO, :]
    for j in range(N_BRANCH):
        lo = j * D_MODEL
        gate_ref[:, lo:lo + D_MODEL] = _sigmoid(
            seg(OFF_GATE + lo, D_MODEL) + gb_ref[:, lo:lo + D_MODEL]).astype(gate_ref.dtype)
    seg_t(wqa_ref, qa_ref, QK_SCALE)
    seg_t(wva_ref, va_ref, 1.0)
    seg_t(wqb_ref, qb_ref, QK_SCALE)
    seg_t(wvb_ref, vb_ref, 1.0)
    ka_ref[...] = seg(OFF_SBK, ATT_W).astype(BF16)
    kb_ref[...] = seg(OFF_MBK, ATT_W).astype(BF16)


def _proj(h, g, w, gb, wt_qa, wt_va, wt_qb, wt_vb, conv_w, conv_b, conv_g, conv_beta, bsz, seq):
    n = h.shape[0]
    tiles_per_seq = seq // ROW_TILE
    sub = ROW_TILE // ATT_TILE
    feat = pl.BlockSpec((1, sub, ATT_W, ATT_TILE),
                        lambda i: (i // tiles_per_seq, i % tiles_per_seq, 0, 0))
    tok_shape = jax.ShapeDtypeStruct((n, ATT_W), BF16)
    feat_shape = jax.ShapeDtypeStruct((bsz, seq // ATT_TILE, ATT_W, ATT_TILE), BF16)
    wt = _resident((ATT_W, D_MODEL))
    vec = _resident((1, CONV_CH))
    return pl.pallas_call(
        functools.partial(_proj_kernel, tiles_per_seq=tiles_per_seq),
        grid=(n // ROW_TILE,),
        in_specs=[_row(D_MODEL), _resident((1, D_MODEL)), _resident((D_MODEL, IN_COLS)),
                  _resident((1, N_BRANCH * D_MODEL)), wt, wt, wt, wt,
                  _resident((CONV_WIDTH, CONV_CH)), vec, vec, vec],
        out_specs=[feat, _row(ATT_W), feat, feat, _row(ATT_W), feat, _row(CONV_CH),
                   _row(N_BRANCH * D_MODEL)],
        out_shape=[feat_shape, tok_shape, feat_shape, feat_shape, tok_shape, feat_shape,
                   jax.ShapeDtypeStruct((n, CONV_CH), BF16),
                   jax.ShapeDtypeStruct((n, N_BRANCH * D_MODEL), BF16)],
        scratch_shapes=[pltpu.VMEM((CONV_HALO + ROW_TILE, CONV_CH), F32)],
        compiler_params=_params(("arbitrary",)),
        name="proj",
    )(h, g, w, gb, wt_qa, wt_va, wt_qb, wt_vb, conv_w, conv_b, conv_g, conv_beta)


def _head_lanes(shape, hh):
    lane = lax.broadcasted_iota(jnp.int32, shape, len(shape) - 1)
    return (lane >= hh * HEAD_DIM) & (lane < (hh + 1) * HEAD_DIM)


def _head_rows(shape, hh):
    row = lax.broadcasted_iota(jnp.int32, shape, 0)
    return (row >= hh * HEAD_DIM) & (row < (hh + 1) * HEAD_DIM)


def _split_bf16(x):
    hi = x.astype(BF16)
    return hi, (x - hi.astype(F32)).astype(BF16)


def _mm(a, b):
    return jnp.dot(a, b, preferred_element_type=F32)


def _att_call(body, name, seq, bsz, extra_in=(), extra_specs=(), scratch=()):
    n_blk = seq // ATT_TILE
    width = PAIRS_PER_STEP * LANES
    feat = pl.BlockSpec((1, n_blk, width, ATT_TILE), lambda b, p: (b, 0, p, 0))
    tok = pl.BlockSpec((1, seq, width), lambda b, p: (b, 0, p))
    return pl.pallas_call(
        functools.partial(body, seq=seq),
        grid=(bsz, N_HEADS // (PAIR * PAIRS_PER_STEP)),
        in_specs=list(extra_specs) + [feat, tok, feat],
        out_specs=tok,
        out_shape=jax.ShapeDtypeStruct((bsz, seq, ATT_W), BF16),
        scratch_shapes=list(scratch),
        compiler_params=_params(("parallel", "parallel")),
        name=name,
    )


def _acc_scratch():
    return pltpu.VMEM((len(_chains()), LANES, ATT_TILE), F32)


def _chains():
    return [(pp, hh, half) for half in range(Q_HALVES) for pp in range(PAIRS_PER_STEP)
            for hh in range(PAIR)]


def _pair_lanes(pp):
    return slice(pp * LANES, (pp + 1) * LANES)


def _store_pair(o_ref, qb, pp, out_t):
    t = ATT_TILE
    o_t = jnp.where(_head_rows((LANES, t), 0), out_t[0], out_t[1])
    q0 = pl.multiple_of(qb * t, t)
    o_ref[0, pl.ds(q0, t), _pair_lanes(pp)] = o_t.T.astype(o_ref.dtype)


def _sb_kernel(qt_ref, k_ref, vt_ref, o_ref, acc_ref, *, seq):
    t = ATT_TILE
    n_sup = seq // (t * Q_HALVES)
    past = (lax.broadcasted_iota(jnp.int32, (t, t), 0)
            < lax.broadcasted_iota(jnp.int32, (t, t), 1))
    r = lax.broadcasted_iota(jnp.int32, (SUFFIX_TILE, SUFFIX_TILE), 0)
    c = lax.broadcasted_iota(jnp.int32, (SUFFIX_TILE, SUFFIX_TILE), 1)
    upper = (r <= c).astype(BF16)
    upper2 = jnp.concatenate([upper, upper], axis=1)
    n_sub = t // SUFFIX_TILE

    def scores(cs, qs, kbs):
        k_t = {(pp, half): k_ref[0, pl.ds(pl.multiple_of(kbs[half] * t, t), t), _pair_lanes(pp)]
               for pp, half in sorted({(c[0], c[2]) for c in cs})}
        return [_mm(k_t[c[0], c[2]], q) for c, q in zip(cs, qs)]

    def weights(zs, carries, diags):
        stacks = []
        for z, diag in zip(zs, diags):
            sp = jnp.maximum(z, 0.0) + jnp.log(1.0 + jnp.exp(-jnp.abs(z)))
            if diag:
                sp = jnp.where(past, sp, 0.0)
            hi, lo = _split_bf16(sp)
            stacks.append([jnp.concatenate([hi[s * SUFFIX_TILE:(s + 1) * SUFFIX_TILE],
                                            lo[s * SUFFIX_TILE:(s + 1) * SUFFIX_TILE]], axis=0)
                           for s in range(n_sub)])
        incls = [[_mm(upper2, st) for st in stack] for stack in stacks]
        ws, new_carries = [], []
        for z, incl, carry, diag in zip(zs, incls, carries, diags):
            parts = [None] * n_sub
            for s in reversed(range(n_sub)):
                parts[s] = jnp.exp(z[s * SUFFIX_TILE:(s + 1) * SUFFIX_TILE] - incl[s] - carry)
                carry = carry + incl[s][0:1, :]
            w = jnp.concatenate(parts, axis=0)
            ws.append((jnp.where(past, w, 0.0) if diag else w).astype(BF16))
            new_carries.append(carry)
        return ws, new_carries

    chains = _chains()

    def values(ws, idx, kbs, first=False):
        for n, w in zip(idx, ws):
            pp, _, half = chains[n]
            pv = _mm(vt_ref[0, kbs[half], _pair_lanes(pp), :], w)
            acc_ref[n] = pv if first else acc_ref[n] + pv

    def unsaturated(carries):
        return (jnp.min(functools.reduce(jnp.minimum, carries)) < SATURATED).astype(jnp.int32)

    assert Q_HALVES == 2
    last = [n for n, c in enumerate(chains) if c[2] == 1]

    def sup_block(sb, _):
        qbs = [sb * Q_HALVES + i for i in range(Q_HALVES)]
        qs = []
        for pp, hh, half in chains:
            q_pair = qt_ref[0, qbs[half], _pair_lanes(pp), :]
            qs.append(jnp.where(_head_rows((LANES, t), hh), q_pair, jnp.zeros_like(q_pair)))
        every = range(len(chains))
        ws, carries = weights(scores(chains, qs, qbs), [jnp.zeros((1, t), F32)] * len(chains),
                              [True] * len(chains))
        values(ws, every, qbs, first=True)

        def body(st):
            j, carries = st[0], st[2:]
            kbs = [qb - j for qb in qbs]
            ws, carries = weights(scores(chains, qs, kbs), carries, [False] * len(chains))
            values(ws, every, kbs)
            return (j + 1, unsaturated(carries), *carries)

        st = lax.while_loop(lambda st: (st[0] <= qbs[0]) & (st[1] > 0), body,
                            (jnp.int32(1), unsaturated(carries), *carries))

        @pl.when((st[0] == qbs[1]) & (unsaturated([st[2 + n] for n in last]) > 0))
        def _():
            ws, _ = weights(scores([chains[n] for n in last], [qs[n] for n in last], [0, 0]),
                            [st[2 + n] for n in last], [False] * len(last))
            values(ws, last, [0, 0])

        for half in range(Q_HALVES):
            for pp in range(PAIRS_PER_STEP):
                _store_pair(o_ref, qbs[half], pp,
                            [acc_ref[chains.index((pp, hh, half))] for hh in range(PAIR)])
        return 0

    lax.fori_loop(0, n_sup, sup_block, 0)


def _sb_attention(qt, k, vt):
    bsz, seq, _ = k.shape
    return _att_call(_sb_kernel, "stickbreak", seq, bsz, scratch=[_acc_scratch()])(qt, k, vt)


def _moba_kernel(slope_ref, qt_ref, k_ref, vt_ref, o_ref, acc_ref, spre_ref, plate_ref, vaug_ref,
                 kx_ref, *, seq):
    t = ATT_TILE
    n_blk = seq // t
    n_sup = n_blk // Q_HALVES
    km_rows = max(n_blk, SUBLANES_BF16)
    causal = (lax.broadcasted_iota(jnp.int32, (t, t), 0)
              <= lax.broadcasted_iota(jnp.int32, (t, t), 1))
    klane = lax.broadcasted_iota(jnp.int32, (t, LANES), 1)
    krow = lax.broadcasted_iota(jnp.int32, (t, LANES), 0).astype(F32)
    qrow = lax.broadcasted_iota(jnp.int32, (LANES, t), 0)
    qcol = lax.broadcasted_iota(jnp.int32, (LANES, t), 1).astype(F32)
    cand = lax.broadcasted_iota(jnp.int32, (n_blk, t), 0)
    extra0 = [(1 - hh) * HEAD_DIM for hh in range(PAIR)]
    slopes, km_split = {}, {}
    for pp in range(PAIRS_PER_STEP):
        kmean = jnp.concatenate(
            [jnp.mean(k_ref[0, j * t:(j + 1) * t, _pair_lanes(pp)].astype(F32), axis=0,
                      keepdims=True) for j in range(n_blk)]
            + ([jnp.zeros((km_rows - n_blk, LANES), F32)] if km_rows > n_blk else []), axis=0)
        for hh in range(PAIR):
            slope = slope_ref[(pl.program_id(1) * PAIRS_PER_STEP + pp) * PAIR + hh]
            slopes[pp, hh] = slope
            km_split[pp, hh] = _split_bf16(
                jnp.where(_head_lanes((km_rows, LANES), hh), kmean, 0.0))
            x0 = extra0[hh]
            k_fill = jnp.where(klane == x0 + n_blk, 1.0,
                               jnp.where(klane == x0 + n_blk + 1, slope * krow, 0.0)).astype(BF16)
            for j in range(n_blk):
                vaug_ref[pp * PAIR + hh, j] = jnp.where(
                    qrow == x0, jnp.ones((), BF16), vt_ref[0, j, _pair_lanes(pp), :])
                kx_ref[pp * PAIR + hh, j] = jnp.where(
                    _head_lanes((t, LANES), hh), k_ref[0, j * t:(j + 1) * t, _pair_lanes(pp)],
                    jnp.where(klane == x0 + j, jnp.full((), -MASK_BIG, BF16), k_fill))

    def q_with_extras(pp, hh, qb):
        q_pair = qt_ref[0, qb, _pair_lanes(pp), :]
        hi, lo = km_split[pp, hh]
        gate = (_mm(hi, q_pair) + _mm(lo, q_pair))[:n_blk]
        valid = cand < qb
        gate = jnp.where(valid, gate, -jnp.inf)
        rank = jnp.zeros((n_blk, t), jnp.int32)
        for j in range(n_blk):
            other = gate[j:j + 1, :]
            ahead = (other > gate) | ((other == gate) & (j < cand))
            rank = rank + ahead.astype(jnp.int32)
        unsel = jnp.where((valid & (rank < MOBA_TOPK)) | (cand == qb), 0.0, 1.0)
        x0 = extra0[hh]
        pieces = [unsel, jnp.zeros((LANES - x0 - n_blk, t), F32)]
        if x0:
            pieces.insert(0, jnp.zeros((x0, t), F32))
        extras = jnp.concatenate(pieces, axis=0)
        extras = jnp.where(qrow == x0 + n_blk, -slopes[pp, hh] * qcol,
                           jnp.where(qrow == x0 + n_blk + 1, 1.0, extras))
        return jnp.where(_head_rows((LANES, t), hh), q_pair, extras.astype(BF16))

    def scores(chains, qs, kb):
        k_x = {ph: kx_ref[ph[0] * PAIR + ph[1], kb] for ph in sorted({c[:2] for c in chains})}
        return [_mm(k_x[c[:2]], q) for c, q in zip(chains, qs)]

    def probs(chains, ss, qbs, kb, ms, diags):
        ps, alphas, new_ms = [], [], []
        for (pp, hh, half), s, m, diag in zip(chains, ss, ms, diags):
            if diag:
                s = jnp.where(causal, s, NEG)
                shift = 0.0
            else:
                shift = slopes[pp, hh] * jnp.asarray((qbs[half] - kb) * t, F32)
            m_new = jnp.maximum(m, jnp.max(s, axis=0, keepdims=True) - shift)
            alphas.append(jnp.exp(m - m_new))
            ps.append(jnp.exp(s - (m_new + shift)).astype(BF16))
            new_ms.append(m_new)
        return ps, alphas, new_ms

    chains = _chains()

    def values(ps, alphas, idx, kb, first=None):
        for i, (n, p, alpha) in enumerate(zip(idx, ps, alphas)):
            pv = _mm(vaug_ref[chains[n][0] * PAIR + chains[n][1], kb], p)
            acc_ref[n] = pv if first is not None and first[i] else alpha * acc_ref[n] + pv

    def sup_block(sb, _):
        qbs = [sb * Q_HALVES + i for i in range(Q_HALVES)]
        q = {c: q_with_extras(c[0], c[1], qbs[c[2]]) for c in chains}
        m = {c: jnp.full((1, t), NEG, F32) for c in chains}
        for kk in reversed(range(Q_HALVES)):
            active = [c for c in chains if c[2] >= kk]
            diags = [c[2] == kk for c in active]
            ss = scores(active, [q[c] for c in active], qbs[kk])
            ps, alphas, ms = probs(active, ss, qbs, qbs[kk], [m[c] for c in active], diags)
            m.update(zip(active, ms))
            values(ps, alphas, [chains.index(c) for c in active], qbs[kk], first=diags)

        subs = [[c for c in chains if c[0] == pp] for pp in range(PAIRS_PER_STEP)]
        n_sub = len(subs[0])
        n_now = n_sub - N_DEFER

        def prefetch(pp, kb):
            sub = subs[pp][:N_PREFETCH]
            ss = scores(sub, [q[c] for c in sub], jnp.maximum(kb, 0))
            for j, s in enumerate(ss):
                spre_ref[pp * N_PREFETCH + j] = s

        def late(pp):
            return [plate_ref[pp * N_DEFER + j] for j in range(N_DEFER)]

        for pp in range(PAIRS_PER_STEP):
            prefetch(pp, qbs[0] - 1)
            for j in range(N_DEFER):
                plate_ref[pp * N_DEFER + j] = jnp.zeros((t, t), BF16)
        finals = []
        for pp, sub in enumerate(subs):
            idx = [chains.index(c) for c in sub]
            qs = [q[c] for c in sub]

            def body(i, st, pp=pp, sub=sub, idx=idx, qs=qs):
                kb = qbs[0] - 1 - i
                ms, alphas_late = list(st[:n_sub]), list(st[n_sub:])
                ss = ([spre_ref[pp * N_PREFETCH + j] for j in range(N_PREFETCH)]
                      + scores(sub[N_PREFETCH:], qs[N_PREFETCH:], kb))
                prefetch(pp, kb - 1)
                values(late(pp), alphas_late, idx[n_now:], kb + 1)
                ps, alphas, ms = probs(sub, ss, qbs, kb, ms, [False] * n_sub)
                values(ps[:n_now], alphas[:n_now], idx[:n_now], kb)
                for j in range(N_DEFER):
                    plate_ref[pp * N_DEFER + j] = ps[n_now + j]
                return (*ms, *alphas[n_now:])

            st = lax.fori_loop(0, qbs[0], body,
                               (*[m[c] for c in sub], *[jnp.ones((1, t), F32)] * N_DEFER))
            finals.append((idx[n_now:], st[n_sub:]))
        for pp, (idx_late, alphas_late) in enumerate(finals):
            values(late(pp), alphas_late, idx_late, 0)
        for half in range(Q_HALVES):
            for pp in range(PAIRS_PER_STEP):
                outs = []
                for hh in range(PAIR):
                    acc = acc_ref[chains.index((pp, hh, half))]
                    outs.append(acc / acc[extra0[hh]:extra0[hh] + 1, :])
                _store_pair(o_ref, qbs[half], pp, outs)
        return 0

    lax.fori_loop(0, n_sup, sup_block, 0)


def _moba_attention(qt, k, vt, slopes):
    bsz, seq, _ = k.shape
    n_blk = seq // MOBA_BLOCK
    assert ATT_TILE == MOBA_BLOCK and HEAD_DIM % n_blk == 0 and n_blk + 2 <= HEAD_DIM
    call = _att_call(_moba_kernel, "moba", seq, bsz,
                     extra_specs=[pl.BlockSpec(memory_space=pltpu.SMEM)],
                     scratch=[_acc_scratch(),
                              pltpu.VMEM((PAIRS_PER_STEP * N_PREFETCH, ATT_TILE, ATT_TILE), F32),
                              pltpu.VMEM((PAIRS_PER_STEP * N_DEFER, ATT_TILE, ATT_TILE), BF16),
                              pltpu.VMEM((PAIRS_PER_STEP * PAIR, n_blk, LANES, ATT_TILE), BF16),
                              pltpu.VMEM((PAIRS_PER_STEP * PAIR, n_blk, ATT_TILE, LANES), BF16)])
    return call(slopes, qt, k, vt)


def kernel(x, ffn1_norm, ffn1_w_in, ffn1_w_out, mix_norm, w_in, gate_bias, sb_w_out, mb_w_out,
           conv_dw, conv_dw_bias, conv_ln_g, conv_ln_b, conv_w_out, w_o, ffn2_norm, ffn2_w_in,
           ffn2_w_out, final_norm):
    bsz, seq, _ = x.shape
    n = bsz * seq
    depth = w_in.shape[0]
    assert seq % ROW_TILE == 0 and seq % (ATT_TILE * Q_HALVES) == 0 and seq % CONV_TILE == 0
    slopes_np = np.exp2(-8.0 * np.arange(1, N_HEADS + 1, dtype=np.float32) / N_HEADS)
    assert all(math.frexp(float(s))[0] == 0.5 for s in slopes_np) and ATT_TILE <= 256
    slopes = jnp.asarray(slopes_np)

    def vec(a):
        return a.reshape(1, -1)

    def seq3(a):
        return a.reshape(bsz, seq, a.shape[-1])

    h = x.reshape(n, D_MODEL)
    for l in range(depth):
        h = _ffn(h, vec(ffn1_norm[l]), ffn1_w_in[l].astype(BF16), ffn1_w_out[l].astype(BF16))
        w_l = w_in[l].astype(BF16)
        wt = [w_l[:, off:off + ATT_W].T for off in (OFF_SBQ, OFF_SBV, OFF_MBQ, OFF_MBV)]
        qa_t, ka, va_t, qb_t, kb, vb_t, oc, gate = _proj(
            h, vec(mix_norm[l]), w_l, vec(gate_bias[l]), *wt,
            conv_dw[l].reshape(CONV_WIDTH, CONV_CH), vec(conv_dw_bias[l]), vec(conv_ln_g[l]),
            vec(conv_ln_b[l]), bsz, seq)
        oa = _sb_attention(qa_t, seq3(ka), va_t)
        ob = _moba_attention(qb_t, seq3(kb), vb_t, slopes)
        last = l == depth - 1
        h = _mix_ffn(h, oa.reshape(n, ATT_W), ob.reshape(n, ATT_W), oc, gate,
                     sb_w_out[l].astype(BF16), mb_w_out[l].astype(BF16),
                     conv_w_out[l].astype(BF16), w_o[l].astype(BF16), vec(ffn2_norm[l]),
                     ffn2_w_in[l].astype(BF16), ffn2_w_out[l].astype(BF16),
                     vec(final_norm) if last else None)
    return h.reshape(bsz, seq, D_MODEL)
```

```python
import functools
import math

import jax
import jax.numpy as jnp
import numpy as np
from jax import lax
from jax.experimental import pallas as pl
from jax.experimental.pallas import tpu as pltpu

D_MODEL = 1024
HEAD_DIM = 64
N_HEADS = 8
ATT_W = N_HEADS * HEAD_DIM
CONV_CH = D_MODEL // 2
CONV_WIDTH = 31
MOBA_BLOCK = 256
MOBA_TOPK = 3
D_FF = 2816
N_BRANCH = 3
EPS = 1e-6
QK_SCALE = HEAD_DIM ** -0.5

OFF_SBQ = 0
OFF_SBK = OFF_SBQ + ATT_W
OFF_SBV = OFF_SBK + ATT_W
OFF_MBQ = OFF_SBV + ATT_W
OFF_MBK = OFF_MBQ + ATT_W
OFF_MBV = OFF_MBK + ATT_W
OFF_CONV = OFF_MBV + ATT_W
OFF_GATE = OFF_CONV + 2 * CONV_CH
IN_COLS = OFF_GATE + N_BRANCH * D_MODEL

LANES = 128
SUBLANES_BF16 = 16
PAIR = LANES // HEAD_DIM
FF_CHUNK = 256
ROW_TILE = 512
ATT_TILE = 256
Q_HALVES = 2
PAIRS_PER_STEP = 4
SUFFIX_TILE = 128
N_PREFETCH = 2
N_DEFER = 2
CONV_TILE = 128
CONV_HALO = 32
VMEM_LIMIT = 56 * 1024 * 1024
SATURATED = 128.0
MASK_BIG = 2.0 ** 100
NEG = -1e30

F32 = jnp.float32
BF16 = jnp.bfloat16


def _rms(x, g):
    ms = jnp.mean(x * x, axis=-1, keepdims=True)
    return x * lax.rsqrt(ms + EPS) * g


def _sigmoid(x):
    return 0.5 * jnp.tanh(0.5 * x) + 0.5


def _silu(x):
    half = 0.5 * x
    return half * jnp.tanh(half) + half


def _resident(shape):
    nd = len(shape)
    return pl.BlockSpec(shape, lambda *_: (0,) * nd, pipeline_mode=pl.Buffered(1))


def _params(sem):
    return pltpu.CompilerParams(dimension_semantics=sem, vmem_limit_bytes=VMEM_LIMIT)


def _half_step_ffn(x, g_ref, w1_ref, w2_ref, mid_ref):
    xn = _rms(x, g_ref[...]).astype(BF16)
    for c in range(D_FF // FF_CHUNK):
        lo = c * FF_CHUNK
        a = jnp.dot(xn, w1_ref[:, lo:lo + FF_CHUNK], preferred_element_type=F32)
        b = jnp.dot(xn, w1_ref[:, D_FF + lo:D_FF + lo + FF_CHUNK], preferred_element_type=F32)
        mid_ref[:, lo:lo + FF_CHUNK] = (_silu(a) * b).astype(BF16)
    return x + 0.5 * jnp.dot(mid_ref[...], w2_ref[...], preferred_element_type=F32)


def _ffn_kernel(x_ref, g_ref, w1_ref, w2_ref, o_ref, mid_ref):
    o_ref[...] = _half_step_ffn(x_ref[...], g_ref, w1_ref, w2_ref, mid_ref)


def _ffn_specs():
    return [_resident((1, D_MODEL)), _resident((D_MODEL, 2 * D_FF)), _resident((D_FF, D_MODEL))]


def _row(width):
    return pl.BlockSpec((ROW_TILE, width), lambda i: (i, 0))


def _ffn(h, g, w1, w2):
    n = h.shape[0]
    return pl.pallas_call(
        _ffn_kernel,
        grid=(n // ROW_TILE,),
        in_specs=[_row(D_MODEL)] + _ffn_specs(),
        out_specs=_row(D_MODEL),
        out_shape=jax.ShapeDtypeStruct((n, D_MODEL), F32),
        scratch_shapes=[pltpu.VMEM((ROW_TILE, D_FF), BF16)],
        compiler_params=_params(("parallel",)),
        name="ffn",
    )(h, g, w1, w2)


def _mix_ffn_kernel(h_ref, oa_ref, ob_ref, oc_ref, gate_ref, wa_ref, wb_ref, wc_ref, wo_ref,
                    g_ref, w1_ref, w2_ref, *rest, final):
    if final:
        gf_ref, o_ref, mid_ref = rest
    else:
        o_ref, mid_ref = rest
    mixed = None
    for j, (y_ref, w_ref) in enumerate(((oa_ref, wa_ref), (ob_ref, wb_ref), (oc_ref, wc_ref))):
        y = jnp.dot(y_ref[...], w_ref[...], preferred_element_type=F32)
        term = gate_ref[:, j * D_MODEL:(j + 1) * D_MODEL] * y
        mixed = term if mixed is None else mixed + term
    x = h_ref[...] + jnp.dot(mixed.astype(BF16), wo_ref[...], preferred_element_type=F32)
    y = _half_step_ffn(x, g_ref, w1_ref, w2_ref, mid_ref)
    if final:
        y = _rms(y, gf_ref[...])
    o_ref[...] = y


def _mix_ffn(h, oa, ob, oc, gate, wa, wb, wc, wo, g, w1, w2, gf=None):
    n = h.shape[0]
    final = gf is not None
    in_specs = ([_row(D_MODEL), _row(ATT_W), _row(ATT_W), _row(CONV_CH), _row(N_BRANCH * D_MODEL),
                 _resident((ATT_W, D_MODEL)), _resident((ATT_W, D_MODEL)),
                 _resident((CONV_CH, D_MODEL)), _resident((D_MODEL, D_MODEL))] + _ffn_specs())
    args = [h, oa, ob, oc, gate, wa, wb, wc, wo, g, w1, w2]
    if final:
        in_specs.append(_resident((1, D_MODEL)))
        args.append(gf)
    return pl.pallas_call(
        functools.partial(_mix_ffn_kernel, final=final),
        grid=(n // ROW_TILE,),
        in_specs=in_specs,
        out_specs=_row(D_MODEL),
        out_shape=jax.ShapeDtypeStruct((n, D_MODEL), F32),
        scratch_shapes=[pltpu.VMEM((ROW_TILE, D_FF), BF16)],
        compiler_params=_params(("parallel",)),
        name="mix_ffn_final" if final else "mix_ffn",
    )(*args)


def _conv_rows(pad_ref, r0, w_ref, b_ref, g_ref, beta_ref):
    lead = CONV_HALO - (CONV_WIDTH - 1)
    rows = CONV_TILE + CONV_HALO
    chunks = []
    for ch in range(CONV_CH // LANES):
        lanes = slice(ch * LANES, (ch + 1) * LANES)
        x = pad_ref[r0:r0 + rows, lanes]
        acc = jnp.zeros((CONV_TILE, LANES), F32) + b_ref[:, lanes]
        for sub in range(8):
            xs = x if sub == 0 else pltpu.roll(x, rows - sub, axis=0)
            for tap in range(CONV_WIDTH):
                off = tap + lead
                if off % 8 == sub:
                    base = off - sub
                    acc = acc + xs[base:base + CONV_TILE, :] * w_ref[tap:tap + 1, lanes]
        chunks.append(acc)
    acc = jnp.concatenate(chunks, axis=1)
    mu = jnp.mean(acc, axis=-1, keepdims=True)
    d = acc - mu
    var = jnp.mean(d * d, axis=-1, keepdims=True)
    return _silu(d * lax.rsqrt(var + EPS) * g_ref[...] + beta_ref[...]).astype(BF16)


def _proj_kernel(h_ref, g_ref, w_ref, gb_ref, wqa_ref, wva_ref, wqb_ref, wvb_ref,
                 cw_ref, cb_ref, cg_ref, cbeta_ref,
                 qa_ref, ka_ref, va_ref, qb_ref, kb_ref, vb_ref, oc_ref, gate_ref, pad_ref,
                 *, tiles_per_seq):
    u = _rms(h_ref[...], g_ref[...]).astype(BF16)

    def seg(off, width):
        return jnp.dot(u, w_ref[:, off:off + width], preferred_element_type=F32)

    def seg_t(wt_ref, o_ref, scale):
        r = lax.dot_general(wt_ref[...], u, (((1,), (1,)), ((), ())), preferred_element_type=F32)
        r = (r * scale).astype(BF16)
        for j in range(ROW_TILE // ATT_TILE):
            o_ref[0, j] = r[:, j * ATT_TILE:(j + 1) * ATT_TILE]

    @pl.when(pl.program_id(0) % tiles_per_seq == 0)
    def _():
        pad_ref[0:CONV_HALO, :] = jnp.zeros((CONV_HALO, CONV_CH), F32)

    pad_ref[CONV_HALO:, :] = seg(OFF_CONV, CONV_CH) * _sigmoid(seg(OFF_CONV + CONV_CH, CONV_CH))

    for r0 in range(0, ROW_TILE, CONV_TILE):
        oc_ref[r0:r0 + CONV_TILE, :] = _conv_rows(pad_ref, r0, cw_ref, cb_ref, cg_ref, cbeta_ref)
    pad_ref[0:CONV_HALO, :] = pad_ref[ROW_TILE:ROW_TILE + CONV_HALO, :]
    for j in range(N_BRANCH):
        lo = j * D_MODEL
        gate_ref[:, lo:lo + D_MODEL] = _sigmoid(
            seg(OFF_GATE + lo, D_MODEL) + gb_ref[:, lo:lo + D_MODEL]).astype(gate_ref.dtype)
    seg_t(wqa_ref, qa_ref, QK_SCALE)
    seg_t(wva_ref, va_ref, 1.0)
    seg_t(wqb_ref, qb_ref, QK_SCALE)
    seg_t(wvb_ref, vb_ref, 1.0)
    ka_ref[...] = seg(OFF_SBK, ATT_W).astype(BF16)
    kb_ref[...] = seg(OFF_MBK, ATT_W).astype(BF16)


def _proj(h, g, w, gb, wt_qa, wt_va, wt_qb, wt_vb, conv_w, conv_b, conv_g, conv_beta, bsz, seq):
    n = h.shape[0]
    tiles_per_seq = seq // ROW_TILE
    sub = ROW_TILE // ATT_TILE
    feat = pl.BlockSpec((1, sub, ATT_W, ATT_TILE),
                        lambda i: (i // tiles_per_seq, i % tiles_per_seq, 0, 0))
    tok_shape = jax.ShapeDtypeStruct((n, ATT_W), BF16)
    feat_shape = jax.ShapeDtypeStruct((bsz, seq // ATT_TILE, ATT_W, ATT_TILE), BF16)
    wt = _resident((ATT_W, D_MODEL))
    vec = _resident((1, CONV_CH))
    return pl.pallas_call(
        functools.partial(_proj_kernel, tiles_per_seq=tiles_per_seq),
        grid=(n // ROW_TILE,),
        in_specs=[_row(D_MODEL), _resident((1, D_MODEL)), _resident((D_MODEL, IN_COLS)),
                  _resident((1, N_BRANCH * D_MODEL)), wt, wt, wt, wt,
                  _resident((CONV_WIDTH, CONV_CH)), vec, vec, vec],
        out_specs=[feat, _row(ATT_W), feat, feat, _row(ATT_W), feat, _row(CONV_CH),
                   _row(N_BRANCH * D_MODEL)],
        out_shape=[feat_shape, tok_shape, feat_shape, feat_shape, tok_shape, feat_shape,
                   jax.ShapeDtypeStruct((n, CONV_CH), BF16),
                   jax.ShapeDtypeStruct((n, N_BRANCH * D_MODEL), BF16)],
        scratch_shapes=[pltpu.VMEM((CONV_HALO + ROW_TILE, CONV_CH), F32)],
        compiler_params=_params(("arbitrary",)),
        name="proj",
    )(h, g, w, gb, wt_qa, wt_va, wt_qb, wt_vb, conv_w, conv_b, conv_g, conv_beta)


def _head_lanes(shape, hh):
    lane = lax.broadcasted_iota(jnp.int32, shape, len(shape) - 1)
    return (lane >= hh * HEAD_DIM) & (lane < (hh + 1) * HEAD_DIM)


def _head_rows(shape, hh):
    row = lax.broadcasted_iota(jnp.int32, shape, 0)
    return (row >= hh * HEAD_DIM) & (row < (hh + 1) * HEAD_DIM)


def _split_bf16(x):
    hi = x.astype(BF16)
    return hi, (x - hi.astype(F32)).astype(BF16)


def _mm(a, b):
    return jnp.dot(a, b, preferred_element_type=F32)


def _att_call(body, name, seq, bsz, extra_specs=(), scratch=()):
    n_blk = seq // ATT_TILE
    width = PAIRS_PER_STEP * LANES
    feat = pl.BlockSpec((1, n_blk, width, ATT_TILE), lambda b, p: (b, 0, p, 0))
    tok = pl.BlockSpec((1, seq, width), lambda b, p: (b, 0, p))
    return pl.pallas_call(
        functools.partial(body, seq=seq),
        grid=(bsz, N_HEADS // (PAIR * PAIRS_PER_STEP)),
        in_specs=list(extra_specs) + [feat, tok, feat],
        out_specs=tok,
        out_shape=jax.ShapeDtypeStruct((bsz, seq, ATT_W), BF16),
        scratch_shapes=list(scratch),
        compiler_params=_params(("parallel", "parallel")),
        name=name,
    )


def _acc_scratch():
    return pltpu.VMEM((len(_chains()), LANES, ATT_TILE), F32)


def _chains():
    return [(pp, hh, half) for half in range(Q_HALVES) for pp in range(PAIRS_PER_STEP)
            for hh in range(PAIR)]


def _pair_lanes(pp):
    return slice(pp * LANES, (pp + 1) * LANES)


def _store_pair(o_ref, qb, pp, out_t):
    t = ATT_TILE
    o_t = jnp.where(_head_rows((LANES, t), 0), out_t[0], out_t[1])
    q0 = pl.multiple_of(qb * t, t)
    o_ref[0, pl.ds(q0, t), _pair_lanes(pp)] = o_t.T.astype(o_ref.dtype)


def _sb_kernel(qt_ref, k_ref, vt_ref, o_ref, acc_ref, *, seq):
    t = ATT_TILE
    n_sup = seq // (t * Q_HALVES)
    past = (lax.broadcasted_iota(jnp.int32, (t, t), 0)
            < lax.broadcasted_iota(jnp.int32, (t, t), 1))
    r = lax.broadcasted_iota(jnp.int32, (SUFFIX_TILE, SUFFIX_TILE), 0)
    c = lax.broadcasted_iota(jnp.int32, (SUFFIX_TILE, SUFFIX_TILE), 1)
    upper = (r <= c).astype(BF16)
    upper2 = jnp.concatenate([upper, upper], axis=1)
    n_sub = t // SUFFIX_TILE

    def scores(cs, qs, kbs):
        k_t = {(pp, half): k_ref[0, pl.ds(pl.multiple_of(kbs[half] * t, t), t), _pair_lanes(pp)]
               for pp, half in sorted({(c[0], c[2]) for c in cs})}
        return [_mm(k_t[c[0], c[2]], q) for c, q in zip(cs, qs)]

    def weights(zs, carries, diags):
        stacks = []
        for z, diag in zip(zs, diags):
            sp = jnp.maximum(z, 0.0) + jnp.log(1.0 + jnp.exp(-jnp.abs(z)))
            if diag:
                sp = jnp.where(past, sp, 0.0)
            hi, lo = _split_bf16(sp)
            stacks.append([jnp.concatenate([hi[s * SUFFIX_TILE:(s + 1) * SUFFIX_TILE],
                                            lo[s * SUFFIX_TILE:(s + 1) * SUFFIX_TILE]], axis=0)
                           for s in range(n_sub)])
        incls = [[_mm(upper2, st) for st in stack] for stack in stacks]
        ws, new_carries = [], []
        for z, incl, carry, diag in zip(zs, incls, carries, diags):
            parts = [None] * n_sub
            for s in reversed(range(n_sub)):
                parts[s] = jnp.exp(z[s * SUFFIX_TILE:(s + 1) * SUFFIX_TILE] - incl[s] - carry)
                carry = carry + incl[s][0:1, :]
            w = jnp.concatenate(parts, axis=0)
            ws.append((jnp.where(past, w, 0.0) if diag else w).astype(BF16))
            new_carries.append(carry)
        return ws, new_carries

    chains = _chains()

    def values(ws, idx, kbs, first=False):
        for n, w in zip(idx, ws):
            pp, _, half = chains[n]
            pv = _mm(vt_ref[0, kbs[half], _pair_lanes(pp), :], w)
            acc_ref[n] = pv if first else acc_ref[n] + pv

    def unsaturated(carries):
        return (jnp.min(functools.reduce(jnp.minimum, carries)) < SATURATED).astype(jnp.int32)

    assert Q_HALVES == 2
    last = [n for n, c in enumerate(chains) if c[2] == 1]

    def sup_block(sb, _):
        qbs = [sb * Q_HALVES + i for i in range(Q_HALVES)]
        qs = []
        for pp, hh, half in chains:
            q_pair = qt_ref[0, qbs[half], _pair_lanes(pp), :]
            qs.append(jnp.where(_head_rows((LANES, t), hh), q_pair, jnp.zeros_like(q_pair)))
        every = range(len(chains))
        ws, carries = weights(scores(chains, qs, qbs), [jnp.zeros((1, t), F32)] * len(chains),
                              [True] * len(chains))
        values(ws, every, qbs, first=True)

        def body(st):
            j, carries = st[0], st[2:]
            kbs = [qb - j for qb in qbs]
            ws, carries = weights(scores(chains, qs, kbs), carries, [False] * len(chains))
            values(ws, every, kbs)
            return (j + 1, unsaturated(carries), *carries)

        st = lax.while_loop(lambda st: (st[0] <= qbs[0]) & (st[1] > 0), body,
                            (jnp.int32(1), unsaturated(carries), *carries))

        @pl.when((st[0] == qbs[1]) & (unsaturated([st[2 + n] for n in last]) > 0))
        def _():
            ws, _ = weights(scores([chains[n] for n in last], [qs[n] for n in last], [0, 0]),
                            [st[2 + n] for n in last], [False] * len(last))
            values(ws, last, [0, 0])

        for half in range(Q_HALVES):
            for pp in range(PAIRS_PER_STEP):
                _store_pair(o_ref, qbs[half], pp,
                            [acc_ref[chains.index((pp, hh, half))] for hh in range(PAIR)])
        return 0

    lax.fori_loop(0, n_sup, sup_block, 0)


def _sb_attention(qt, k, vt):
    bsz, seq, _ = k.shape
    return _att_call(_sb_kernel, "stickbreak", seq, bsz, scratch=[_acc_scratch()])(qt, k, vt)


def _moba_kernel(slope_ref, qt_ref, k_ref, vt_ref, o_ref, acc_ref, spre_ref, plate_ref, vaug_ref,
                 *, seq):
    t = ATT_TILE
    n_blk = seq // t
    n_sup = n_blk // Q_HALVES
    km_rows = max(n_blk, SUBLANES_BF16)
    causal = (lax.broadcasted_iota(jnp.int32, (t, t), 0)
              <= lax.broadcasted_iota(jnp.int32, (t, t), 1))
    klane = lax.broadcasted_iota(jnp.int32, (t, LANES), 1)
    krow = lax.broadcasted_iota(jnp.int32, (t, LANES), 0).astype(F32)
    qrow = lax.broadcasted_iota(jnp.int32, (LANES, t), 0)
    qcol = lax.broadcasted_iota(jnp.int32, (LANES, t), 1).astype(F32)
    cand = lax.broadcasted_iota(jnp.int32, (n_blk, t), 0)
    extra0 = [(1 - hh) * HEAD_DIM for hh in range(PAIR)]
    slopes, km_split, k_fill = {}, {}, {}
    for pp in range(PAIRS_PER_STEP):
        kmean = jnp.concatenate(
            [jnp.mean(k_ref[0, j * t:(j + 1) * t, _pair_lanes(pp)].astype(F32), axis=0,
                      keepdims=True) for j in range(n_blk)]
            + ([jnp.zeros((km_rows - n_blk, LANES), F32)] if km_rows > n_blk else []), axis=0)
        for hh in range(PAIR):
            slope = slope_ref[(pl.program_id(1) * PAIRS_PER_STEP + pp) * PAIR + hh]
            slopes[pp, hh] = slope
            km_split[pp, hh] = _split_bf16(
                jnp.where(_head_lanes((km_rows, LANES), hh), kmean, 0.0))
            x0 = extra0[hh]
            k_fill[pp, hh] = jnp.where(
                klane == x0 + n_blk, 1.0,
                jnp.where(klane == x0 + n_blk + 1, slope * krow, 0.0)).astype(BF16)
            for j in range(n_blk):
                vaug_ref[pp * PAIR + hh, j] = jnp.where(
                    qrow == x0, jnp.ones((), BF16), vt_ref[0, j, _pair_lanes(pp), :])

    def q_with_extras(pp, hh, qb):
        q_pair = qt_ref[0, qb, _pair_lanes(pp), :]
        hi, lo = km_split[pp, hh]
        gate = (_mm(hi, q_pair) + _mm(lo, q_pair))[:n_blk]
        valid = cand < qb
        gate = jnp.where(valid, gate, -jnp.inf)
        rank = jnp.zeros((n_blk, t), jnp.int32)
        for j in range(n_blk):
            other = gate[j:j + 1, :]
            ahead = (other > gate) | ((other == gate) & (j < cand))
            rank = rank + ahead.astype(jnp.int32)
        unsel = jnp.where(valid & (rank < MOBA_TOPK), 0.0, 1.0)
        x0 = extra0[hh]
        pieces = [unsel, jnp.zeros((LANES - x0 - n_blk, t), F32)]
        if x0:
            pieces.insert(0, jnp.zeros((x0, t), F32))
        extras = jnp.concatenate(pieces, axis=0)
        extras = jnp.where(qrow == x0 + n_blk, -slopes[pp, hh] * qcol,
                           jnp.where(qrow == x0 + n_blk + 1, 1.0, extras))
        return jnp.where(_head_rows((LANES, t), hh), q_pair, extras.astype(BF16))

    def k_with_extras(pp, hh, kb, diag):
        k0 = pl.multiple_of(kb * t, t)
        k_pair = k_ref[0, pl.ds(k0, t), _pair_lanes(pp)]
        fill = k_fill[pp, hh]
        if not diag:
            fill = jnp.where(klane == extra0[hh] + kb, jnp.full((), -MASK_BIG, BF16), fill)
        return jnp.where(_head_lanes((t, LANES), hh), k_pair, fill)

    def scores(chains, qs, kb, diags):
        k_x = {phd: k_with_extras(*phd[:2], kb, phd[2])
               for phd in sorted({(c[0], c[1], d) for c, d in zip(chains, diags)})}
        return [_mm(k_x[c[0], c[1], d], q) for c, q, d in zip(chains, qs, diags)]

    def probs(chains, ss, qbs, kb, ms, diags):
        ps, alphas, new_ms = [], [], []
        for (pp, hh, half), s, m, diag in zip(chains, ss, ms, diags):
            if diag:
                s = jnp.where(causal, s, NEG)
                shift = 0.0
            else:
                shift = slopes[pp, hh] * jnp.asarray((qbs[half] - kb) * t, F32)
            m_new = jnp.maximum(m, jnp.max(s, axis=0, keepdims=True) - shift)
            alphas.append(jnp.exp(m - m_new))
            ps.append(jnp.exp(s - (m_new + shift)).astype(BF16))
            new_ms.append(m_new)
        return ps, alphas, new_ms

    chains = _chains()

    def values(ps, alphas, idx, kb, first=None):
        for i, (n, p, alpha) in enumerate(zip(idx, ps, alphas)):
            pv = _mm(vaug_ref[chains[n][0] * PAIR + chains[n][1], kb], p)
            acc_ref[n] = pv if first is not None and first[i] else alpha * acc_ref[n] + pv

    def sup_block(sb, _):
        qbs = [sb * Q_HALVES + i for i in range(Q_HALVES)]
        q = {c: q_with_extras(c[0], c[1], qbs[c[2]]) for c in chains}
        m = {c: jnp.full((1, t), NEG, F32) for c in chains}
        for kk in reversed(range(Q_HALVES)):
            active = [c for c in chains if c[2] >= kk]
            diags = [c[2] == kk for c in active]
            ss = scores(active, [q[c] for c in active], qbs[kk], diags)
            ps, alphas, ms = probs(active, ss, qbs, qbs[kk], [m[c] for c in active], diags)
            m.update(zip(active, ms))
            values(ps, alphas, [chains.index(c) for c in active], qbs[kk], first=diags)

        subs = [[c for c in chains if c[0] == pp] for pp in range(PAIRS_PER_STEP)]
        n_sub = len(subs[0])
        n_now = n_sub - N_DEFER

        def prefetch(pp, kb):
            sub = subs[pp][:N_PREFETCH]
            ss = scores(sub, [q[c] for c in sub], jnp.maximum(kb, 0), [False] * N_PREFETCH)
            for j, s in enumerate(ss):
                spre_ref[pp * N_PREFETCH + j] = s

        def late(pp):
            return [plate_ref[pp * N_DEFER + j] for j in range(N_DEFER)]

        for pp in range(PAIRS_PER_STEP):
            prefetch(pp, qbs[0] - 1)
            for j in range(N_DEFER):
                plate_ref[pp * N_DEFER + j] = jnp.zeros((t, t), BF16)
        finals = []
        for pp, sub in enumerate(subs):
            idx = [chains.index(c) for c in sub]
            qs = [q[c] for c in sub]

            def body(i, st, pp=pp, sub=sub, idx=idx, qs=qs):
                kb = qbs[0] - 1 - i
                ms, alphas_late = list(st[:n_sub]), list(st[n_sub:])
                ss = ([spre_ref[pp * N_PREFETCH + j] for j in range(N_PREFETCH)]
                      + scores(sub[N_PREFETCH:], qs[N_PREFETCH:], kb, [False] * (n_sub - N_PREFETCH)))
                prefetch(pp, kb - 1)
                values(late(pp), alphas_late, idx[n_now:], kb + 1)
                ps, alphas, ms = probs(sub, ss, qbs, kb, ms, [False] * n_sub)
                values(ps[:n_now], alphas[:n_now], idx[:n_now], kb)
                for j in range(N_DEFER):
                    plate_ref[pp * N_DEFER + j] = ps[n_now + j]
                return (*ms, *alphas[n_now:])

            st = lax.fori_loop(0, qbs[0], body,
                               (*[m[c] for c in sub], *[jnp.ones((1, t), F32)] * N_DEFER))
            finals.append((idx[n_now:], st[n_sub:]))
        for pp, (idx_late, alphas_late) in enumerate(finals):
            values(late(pp), alphas_late, idx_late, 0)
        for half in range(Q_HALVES):
            for pp in range(PAIRS_PER_STEP):
                outs = []
                for hh in range(PAIR):
                    acc = acc_ref[chains.index((pp, hh, half))]
                    outs.append(acc / acc[extra0[hh]:extra0[hh] + 1, :])
                _store_pair(o_ref, qbs[half], pp, outs)
        return 0

    lax.fori_loop(0, n_sup, sup_block, 0)


def _moba_attention(qt, k, vt, slopes):
    bsz, seq, _ = k.shape
    n_blk = seq // MOBA_BLOCK
    assert ATT_TILE == MOBA_BLOCK and HEAD_DIM % n_blk == 0 and n_blk + 2 <= HEAD_DIM
    call = _att_call(_moba_kernel, "moba", seq, bsz,
                     extra_specs=[pl.BlockSpec(memory_space=pltpu.SMEM)],
                     scratch=[_acc_scratch(),
                              pltpu.VMEM((PAIRS_PER_STEP * N_PREFETCH, ATT_TILE, ATT_TILE), F32),
                              pltpu.VMEM((PAIRS_PER_STEP * N_DEFER, ATT_TILE, ATT_TILE), BF16),
                              pltpu.VMEM((PAIRS_PER_STEP * PAIR, n_blk, LANES, ATT_TILE), BF16)])
    return call(slopes, qt, k, vt)


def kernel(x, ffn1_norm, ffn1_w_in, ffn1_w_out, mix_norm, w_in, gate_bias, sb_w_out, mb_w_out,
           conv_dw, conv_dw_bias, conv_ln_g, conv_ln_b, conv_w_out, w_o, ffn2_norm, ffn2_w_in,
           ffn2_w_out, final_norm):
    bsz, seq, _ = x.shape
    n = bsz * seq
    depth = w_in.shape[0]
    assert seq % ROW_TILE == 0 and seq % (ATT_TILE * Q_HALVES) == 0 and seq % CONV_TILE == 0
    slopes_np = np.exp2(-8.0 * np.arange(1, N_HEADS + 1, dtype=np.float32) / N_HEADS)
    assert all(math.frexp(float(s))[0] == 0.5 for s in slopes_np) and ATT_TILE <= 256
    slopes = jnp.asarray(slopes_np)

    def vec(a):
        return a.reshape(1, -1)

    def seq3(a):
        return a.reshape(bsz, seq, a.shape[-1])

    h = x.reshape(n, D_MODEL)
    for l in range(depth):
        h = _ffn(h, vec(ffn1_norm[l]), ffn1_w_in[l].astype(BF16), ffn1_w_out[l].astype(BF16))
        w_l = w_in[l].astype(BF16)
        wt = [w_l[:, off:off + ATT_W].T for off in (OFF_SBQ, OFF_SBV, OFF_MBQ, OFF_MBV)]
        qa_t, ka, va_t, qb_t, kb, vb_t, oc, gate = _proj(
            h, vec(mix_norm[l]), w_l, vec(gate_bias[l]), *wt,
            conv_dw[l].reshape(CONV_WIDTH, CONV_CH), vec(conv_dw_bias[l]), vec(conv_ln_g[l]),
            vec(conv_ln_b[l]), bsz, seq)
        oa = _sb_attention(qa_t, seq3(ka), va_t)
        ob = _moba_attention(qb_t, seq3(kb), vb_t, slopes)
        last = l == depth - 1
        h = _mix_ffn(h, oa.reshape(n, ATT_W), ob.reshape(n, ATT_W), oc, gate,
                     sb_w_out[l].astype(BF16), mb_w_out[l].astype(BF16),
                     conv_w_out[l].astype(BF16), w_o[l].astype(BF16), vec(ffn2_norm[l]),
                     ffn2_w_in[l].astype(BF16), ffn2_w_out[l].astype(BF16),
                     vec(final_norm) if last else None)
    return h.reshape(bsz, seq, D_MODEL)
```
